```python
import math
import jax, jax.numpy as jnp
from jax import lax
import numpy as np

D_MODEL = 2048
BATCH = 8
SEQ = 4096
DEPTH = 4
DEC_BATCH = 8
DEC_SEQ = 2048
PAST_LEN = 128

GRID_W = 64
BLOCK = 128
EPS = 1e-6
ROPE_THETA = 10000.0
NEG = -1e30
MLA_HEADS = 8
MLA_NOPE = 128
MLA_ROPE = 64
MLA_QK = MLA_NOPE + MLA_ROPE
MLA_V = 128
Q_LORA = 512
KV_LORA = 256
MLA_W = MLA_HEADS * MLA_V
AX_HEADS = 8
AX_KV = 2
AX_HD = 128
AX_W = AX_HEADS * AX_HD
WIN_HEADS = 8
WIN_KV = 2
WIN_HD = 128
WINDOW = 128
WIN_W = WIN_HEADS * WIN_HD
N_BUCKETS = 32
MAX_DIST = 128
D_IN = (Q_LORA + KV_LORA + MLA_ROPE + MLA_W
        + AX_W + 2 * AX_KV * AX_HD + AX_W
        + WIN_W + 2 * WIN_KV * WIN_HD + WIN_W
        + 3 * D_MODEL)

kernel_name = 'hybrid_gated_mla_axial_window_encoder'


def _split_points():
    sizes = [Q_LORA, KV_LORA, MLA_ROPE, MLA_W,
             AX_W, AX_KV * AX_HD, AX_KV * AX_HD, AX_W,
             WIN_W, WIN_KV * WIN_HD, WIN_KV * WIN_HD, WIN_W,
             D_MODEL, D_MODEL, D_MODEL]
    pts, acc = [], 0
    for s in sizes[:-1]:
        acc += s
        pts.append(acc)
    return pts


def rmsnorm(x, g):
    xf = x.astype(jnp.float32)
    y = xf * lax.rsqrt(jnp.mean(xf * xf, axis=-1, keepdims=True) + EPS)
    return (y * g.astype(jnp.float32)).astype(x.dtype)


def rope(x, pos):
    half = x.shape[-1] // 2
    freqs = ROPE_THETA ** (-jnp.arange(half, dtype=jnp.float32) / half)
    ang = pos[:, None] * freqs[None, :]
    cos = jnp.cos(ang)[None, :, None, :]
    sin = jnp.sin(ang)[None, :, None, :]
    xf = x.astype(jnp.float32)
    x1, x2 = xf[..., :half], xf[..., half:]
    return jnp.concatenate([x1 * cos - x2 * sin, x2 * cos + x1 * sin], axis=-1).astype(x.dtype)


def dense_block_attn(q, k, v, n_kv):
    B, S, H, Dk = q.shape
    G = H // n_kv
    nb = S // BLOCK
    Dv = v.shape[-1]
    scale = Dk ** -0.5
    qb = q.reshape(B, nb, BLOCK, n_kv, G, Dk).transpose(1, 0, 2, 3, 4, 5)

    def one(qblk):
        s = jnp.einsum('bqkgd,bskd->bkgqs', qblk, k).astype(jnp.float32) * scale
        p = jax.nn.softmax(s, axis=-1).astype(v.dtype)
        return jnp.einsum('bkgqs,bskd->bqkgd', p, v)

    o = lax.map(one, qb)
    return o.transpose(1, 0, 2, 3, 4, 5).reshape(B, S, H * Dv)


def t5_bucket(rel):
    half = N_BUCKETS // 2
    max_exact = half // 2
    ret = jnp.where(rel > 0, half, 0)
    n = jnp.abs(rel)
    nf = jnp.maximum(n, 1).astype(jnp.float32)
    large = max_exact + (jnp.log(nf / max_exact) / math.log(MAX_DIST / max_exact)
                         * (half - max_exact)).astype(jnp.int32)
    large = jnp.minimum(large, half - 1)
    return ret + jnp.where(n < max_exact, n, large)


def window_attn(q, k, v, sink, rel_bias):
    B, S, H, D = q.shape
    KV = k.shape[2]
    G = H // KV
    nb = S // BLOCK
    span = BLOCK + 2 * WINDOW
    qb = q.reshape(B, nb, BLOCK, KV, G, D)
    pad = ((0, 0), (WINDOW, WINDOW), (0, 0), (0, 0))
    kp = jnp.pad(k, pad)
    vp = jnp.pad(v, pad)
    idx = (jnp.arange(nb) * BLOCK)[:, None] + jnp.arange(span)[None, :]
    kb = kp[:, idx]
    vb = vp[:, idx]
    s = jnp.einsum('bnqkgd,bnskd->bnkgqs', qb, kb).astype(jnp.float32) * (D ** -0.5)
    rel = jnp.arange(span)[None, :] - WINDOW - jnp.arange(BLOCK)[:, None]
    bias = rel_bias[t5_bucket(rel)].astype(jnp.float32)
    bias = bias.transpose(2, 0, 1).reshape(KV, G, BLOCK, span)
    kpos = idx - WINDOW
    valid = (jnp.abs(rel) <= WINDOW)[None] & ((kpos >= 0) & (kpos < S))[:, None, :]
    s = jnp.where(valid[None, :, None, None], s + bias, NEG)
    sk = jnp.broadcast_to(sink.astype(jnp.float32).reshape(KV, G, 1, 1), (B, nb, KV, G, BLOCK, 1))
    p = jax.nn.softmax(jnp.concatenate([s, sk], axis=-1), axis=-1)[..., :span].astype(v.dtype)
    o = jnp.einsum('bnkgqs,bnskd->bnqkgd', p, vb)
    return o.reshape(B, S, H * D)


def layer(x, c, ln_g, w_ada, b_ada, w_in, q_a_norm, w_q_up, kv_a_norm, w_kv_up,
          mla_qn, mla_kn, ax_qn, ax_kn, win_qn, win_kn, win_sink, rel_bias,
          w_br_mla, w_br_ax, w_br_win, w_out):
    B, S, _ = x.shape
    rows = S // GRID_W
    mod = c @ w_ada + b_ada
    shift = mod[:, None, :D_MODEL]
    scale = mod[:, None, D_MODEL:2 * D_MODEL]
    gate = mod[:, None, 2 * D_MODEL:]
    h = rmsnorm(x, ln_g) * (1.0 + scale) + shift
    proj = h @ w_in
    (cq, ckv, kr, g_mla, aq, ak, av, g_ax, wq, wk, wv, g_win,
     m_mla, m_ax, m_win) = jnp.split(proj, _split_points(), axis=-1)
    pos = jnp.arange(S, dtype=jnp.float32)
    row = jnp.repeat(jnp.arange(rows, dtype=jnp.float32), GRID_W)
    col = jnp.tile(jnp.arange(GRID_W, dtype=jnp.float32), rows)

    q = (rmsnorm(cq, q_a_norm) @ w_q_up).reshape(B, S, MLA_HEADS, MLA_QK)
    kv = (rmsnorm(ckv, kv_a_norm) @ w_kv_up).reshape(B, S, MLA_HEADS, MLA_NOPE + MLA_V)
    k_nope, v_mla = kv[..., :MLA_NOPE], kv[..., MLA_NOPE:]
    k_pe = jnp.broadcast_to(kr[:, :, None, :], (B, S, MLA_HEADS, MLA_ROPE))
    k = jnp.concatenate([k_nope, k_pe], axis=-1)
    q = rmsnorm(q, mla_qn)
    k = rmsnorm(k, mla_kn)
    q = jnp.concatenate([q[..., :MLA_NOPE], rope(q[..., MLA_NOPE:], pos)], axis=-1)
    k = jnp.concatenate([k[..., :MLA_NOPE], rope(k[..., MLA_NOPE:], pos)], axis=-1)
    o_mla = dense_block_attn(q, k, v_mla, MLA_HEADS)

    hd2 = AX_HD // 2
    aq = rmsnorm(aq.reshape(B, S, AX_HEADS, AX_HD), ax_qn)
    ak = rmsnorm(ak.reshape(B, S, AX_KV, AX_HD), ax_kn)
    av = av.reshape(B, S, AX_KV, AX_HD)
    aq = jnp.concatenate([rope(aq[..., :hd2], row), rope(aq[..., hd2:], col)], axis=-1)
    ak = jnp.concatenate([rope(ak[..., :hd2], row), rope(ak[..., hd2:], col)], axis=-1)
    o_ax = dense_block_attn(aq, ak, av, AX_KV)

    wq = rmsnorm(wq.reshape(B, S, WIN_HEADS, WIN_HD), win_qn)
    wk = rmsnorm(wk.reshape(B, S, WIN_KV, WIN_HD), win_kn)
    wv = wv.reshape(B, S, WIN_KV, WIN_HD)
    o_win = window_attn(wq, wk, wv, win_sink, rel_bias)

    y = (jax.nn.sigmoid(m_mla) * ((o_mla * jax.nn.silu(g_mla)) @ w_br_mla)
         + jax.nn.sigmoid(m_ax) * ((o_ax * jax.nn.silu(g_ax)) @ w_br_ax)
         + jax.nn.sigmoid(m_win) * ((o_win * jax.nn.silu(g_win)) @ w_br_win))
    return x + gate * (y @ w_out)


def setup_inputs(seed: int = 0) -> dict:
    key = jax.random.key(seed)
    ks = jax.random.split(key, 24)
    f32 = jnp.float32

    def nrm(k, shape, s):
        return jax.random.normal(k, shape, f32) * s

    def gain(k, shape):
        return 1.0 + 0.02 * jax.random.normal(k, shape, f32)

    return {
        'x_prompt': nrm(ks[0], (BATCH, SEQ, D_MODEL), 1.0),
        'x_sample': nrm(ks[1], (DEC_BATCH, DEC_SEQ, D_MODEL), 1.0),
        'c_prompt': nrm(ks[2], (BATCH, D_MODEL), 1.0),
        'c_sample': nrm(ks[3], (DEC_BATCH, D_MODEL), 1.0),
        'ln_g': gain(ks[4], (DEPTH, D_MODEL)),
        'w_ada': nrm(ks[5], (DEPTH, D_MODEL, 3 * D_MODEL), 0.5 * D_MODEL ** -0.5),
        'b_ada': nrm(ks[6], (DEPTH, 3 * D_MODEL), 0.01),
        'w_in': nrm(ks[7], (DEPTH, D_MODEL, D_IN), D_MODEL ** -0.5),
        'q_a_norm': gain(ks[8], (DEPTH, Q_LORA)),
        'w_q_up': nrm(ks[9], (DEPTH, Q_LORA, MLA_HEADS * MLA_QK), Q_LORA ** -0.5),
        'kv_a_norm': gain(ks[10], (DEPTH, KV_LORA)),
        'w_kv_up': nrm(ks[11], (DEPTH, KV_LORA, MLA_HEADS * (MLA_NOPE + MLA_V)), KV_LORA ** -0.5),
        'mla_qn': gain(ks[12], (DEPTH, MLA_QK)),
        'mla_kn': gain(ks[13], (DEPTH, MLA_QK)),
        'ax_qn': gain(ks[14], (DEPTH, AX_HD)),
        'ax_kn': gain(ks[15], (DEPTH, AX_HD)),
        'win_qn': gain(ks[16], (DEPTH, WIN_HD)),
        'win_kn': gain(ks[17], (DEPTH, WIN_HD)),
        'win_sink': nrm(ks[18], (DEPTH, WIN_HEADS), 1.0),
        'rel_bias': nrm(ks[19], (N_BUCKETS, WIN_HEADS), 0.5),
        'w_br_mla': nrm(ks[20], (DEPTH, MLA_W, D_MODEL), MLA_W ** -0.5),
        'w_br_ax': nrm(ks[21], (DEPTH, AX_W, D_MODEL), AX_W ** -0.5),
        'w_br_win': nrm(ks[22], (DEPTH, WIN_W, D_MODEL), WIN_W ** -0.5),
        'w_out': nrm(ks[23], (DEPTH, D_MODEL, D_MODEL), D_MODEL ** -0.5),
    }


def reference(x_prompt, x_sample, c_prompt, c_sample, ln_g, w_ada, b_ada, w_in,
              q_a_norm, w_q_up, kv_a_norm, w_kv_up, mla_qn, mla_kn, ax_qn, ax_kn,
              win_qn, win_kn, win_sink, rel_bias, w_br_mla, w_br_ax, w_br_win, w_out):
    y_prompt = x_prompt
    y_sample = x_sample
    for l in range(DEPTH):
        p = (ln_g[l], w_ada[l], b_ada[l], w_in[l], q_a_norm[l], w_q_up[l],
             kv_a_norm[l], w_kv_up[l], mla_qn[l], mla_kn[l], ax_qn[l], ax_kn[l],
             win_qn[l], win_kn[l], win_sink[l], rel_bias,
             w_br_mla[l], w_br_ax[l], w_br_win[l], w_out[l])
        y_prompt = layer(y_prompt, c_prompt, *p)
        y_sample = layer(y_sample, c_sample, *p)
    return (y_prompt, y_sample)
```

```python
import functools
import math

import jax
import jax.numpy as jnp
from jax import lax
from jax.experimental import pallas as pl
from jax.experimental.pallas import tpu as pltpu

F32 = jnp.float32
BF16 = jnp.bfloat16

GRID_W = 64
BLOCK = 128
EPS = 1e-6
ROPE_THETA = 10000.0
NEG = -1e30
LOG2E = 1.4426950408889634
LANES = 128

MLA_HEADS = 8
MLA_NOPE = 128
MLA_ROPE = 64
MLA_QK = MLA_NOPE + MLA_ROPE
MLA_V = 128
Q_LORA = 512
KV_LORA = 256
AX_HEADS = 8
AX_KV = 2
AX_HD = 128
WIN_HEADS = 8
WIN_KV = 2
WIN_HD = 128
WINDOW = 128
N_BUCKETS = 32
MAX_DIST = 128
ROPE_HALF = 32

VMEM_LIMIT = 56 * 1024 * 1024


def _cparams(sem):
    return pltpu.CompilerParams(dimension_semantics=sem, vmem_limit_bytes=VMEM_LIMIT)


def _pick(n, pref):
    t = min(pref, n)
    while n % t:
        t //= 2
    return t


def _mod_kernel(c_ref, w_ref, b_ref, o_ref):
    o_ref[0] = jnp.dot(c_ref[...], w_ref[0].astype(BF16), preferred_element_type=F32) + b_ref[0]


def _modulation(c_all, w_ada, b_ada):
    L, D, N = w_ada.shape
    R = c_all.shape[0]
    tn = _pick(N, 512)
    return pl.pallas_call(
        _mod_kernel,
        grid=(L, N // tn),
        in_specs=[pl.BlockSpec((R, D), lambda l, j: (0, 0)),
                  pl.BlockSpec((1, D, tn), lambda l, j: (l, 0, j)),
                  pl.BlockSpec((1, 1, tn), lambda l, j: (l, 0, j))],
        out_specs=pl.BlockSpec((1, R, tn), lambda l, j: (l, 0, j)),
        out_shape=jax.ShapeDtypeStruct((L, R, N), F32),
        compiler_params=_cparams(("parallel", "parallel")),
        name="adaln_mod",
    )(c_all.astype(BF16), w_ada, b_ada.reshape(L, 1, N))


def _h_kernel(x_ref, g_ref, sh_ref, sc_ref, o_ref):
    x = x_ref[0]
    r = lax.rsqrt(jnp.mean(x * x, axis=-1, keepdims=True) + EPS)
    y = x * r * g_ref[...]
    o_ref[0] = (y * (1.0 + sc_ref[0]) + sh_ref[0]).astype(BF16)


def _norm_modulate(x, ln_g, mod):
    B, S, D = x.shape
    ts = _pick(S, 512)
    return pl.pallas_call(
        _h_kernel,
        grid=(B, S // ts),
        in_specs=[pl.BlockSpec((1, ts, D), lambda b, i: (b, i, 0)),
                  pl.BlockSpec((1, D), lambda b, i: (0, 0)),
                  pl.BlockSpec((1, 1, D), lambda b, i: (b, 0, 0)),
                  pl.BlockSpec((1, 1, D), lambda b, i: (b, 0, 1))],
        out_specs=pl.BlockSpec((1, ts, D), lambda b, i: (b, i, 0)),
        out_shape=jax.ShapeDtypeStruct((B, S, D), BF16),
        compiler_params=_cparams(("parallel", "parallel")),
        name="norm_modulate",
    )(x, ln_g.reshape(1, D), mod, mod)


def _swap32(y):
    lane = lax.broadcasted_iota(jnp.int32, y.shape, 1)
    return jnp.where((lane % (2 * ROPE_HALF)) < ROPE_HALF,
                     pltpu.roll(y, LANES - ROPE_HALF, 1), pltpu.roll(y, ROPE_HALF, 1))


def _rope128(y, cos, sin_signed):
    return y * cos + _swap32(y) * sin_signed


def _rope_tables(pos_a, pos_b):
    freqs = ROPE_THETA ** (-jnp.arange(ROPE_HALF, dtype=F32) / ROPE_HALF)

    def seg(pos):
        ang = pos[:, None] * freqs[None, :]
        c, s = jnp.cos(ang), jnp.sin(ang)
        return jnp.concatenate([c, c], axis=-1), jnp.concatenate([-s, s], axis=-1)

    ca, sa = seg(pos_a)
    cb, sb = seg(pos_b)
    return jnp.concatenate([ca, cb], axis=-1), jnp.concatenate([sa, sb], axis=-1)


def _mla_prep_kernel(h_ref, wlat_ref, wq_ref, wkv_ref, gqa_ref, gkva_ref, gq_ref, gk_ref,
                     cos_ref, sin_ref, q_ref, k_ref, v_ref):
    lat = jnp.dot(h_ref[...], wlat_ref[...], preferred_element_type=F32)
    cq = lat[:, :Q_LORA]
    ckv = lat[:, Q_LORA:Q_LORA + KV_LORA]
    kr2 = lat[:, Q_LORA + KV_LORA:]
    cqn = cq * lax.rsqrt(jnp.mean(cq * cq, axis=-1, keepdims=True) + EPS) * gqa_ref[...]
    ckvn = ckv * lax.rsqrt(jnp.mean(ckv * ckv, axis=-1, keepdims=True) + EPS) * gkva_ref[...]
    qraw = jnp.dot(cqn.astype(BF16), wq_ref[...], preferred_element_type=F32)
    kvraw = jnp.dot(ckvn.astype(BF16), wkv_ref[...], preferred_element_type=F32)
    cos = cos_ref[...]
    sin = sin_ref[...]
    lane = lax.broadcasted_iota(jnp.int32, cos.shape, 1)
    lo = lane < MLA_ROPE
    gq = gq_ref[...]
    gk = gk_ref[...]
    nope_w = MLA_HEADS * MLA_NOPE
    inv_qk = 1.0 / MLA_QK
    kr_sq = kr2 * kr2
    ss_kr = jnp.sum(jnp.where(lo, kr_sq, 0.0), axis=-1, keepdims=True)
    k_rope = _rope128(kr2 * gk[:, LANES:], cos, sin)
    for p in range(MLA_HEADS // 2):
        qr = qraw[:, nope_w + LANES * p: nope_w + LANES * (p + 1)]
        qr_sq = qr * qr
        ss_r = (jnp.sum(jnp.where(lo, qr_sq, 0.0), axis=-1, keepdims=True),
                jnp.sum(jnp.where(lo, 0.0, qr_sq), axis=-1, keepdims=True))
        q_rope = _rope128(qr * gq[:, LANES:], cos, sin)
        for e in range(2):
            hd = 2 * p + e
            keep = lo if e == 0 else jnp.logical_not(lo)
            qn = qraw[:, LANES * hd: LANES * (hd + 1)]
            rq = lax.rsqrt((jnp.sum(qn * qn, axis=-1, keepdims=True) + ss_r[e]) * inv_qk + EPS)
            q_ref[:, 2 * LANES * hd: 2 * LANES * hd + LANES] = (qn * rq * gq[:, :LANES]).astype(BF16)
            q_ref[:, 2 * LANES * hd + LANES: 2 * LANES * (hd + 1)] = jnp.where(keep, q_rope * rq, 0.0).astype(BF16)
            kn = kvraw[:, LANES * hd: LANES * (hd + 1)]
            rk = lax.rsqrt((jnp.sum(kn * kn, axis=-1, keepdims=True) + ss_kr) * inv_qk + EPS)
            k_ref[:, 2 * LANES * hd: 2 * LANES * hd + LANES] = (kn * rk * gk[:, :LANES]).astype(BF16)
            k_ref[:, 2 * LANES * hd + LANES: 2 * LANES * (hd + 1)] = jnp.where(keep, k_rope * rk, 0.0).astype(BF16)
    v_ref[...] = kvraw[:, nope_w:].astype(BF16)


def _mla_prep(h2, wlat, wq, wkv, gqa, gkva, gq, gk, cos, sin, S):
    T, D = h2.shape
    tm = _pick(S, 512)
    nS = S // tm
    nlat = wlat.shape[1]
    hq = MLA_HEADS * 2 * LANES
    const = lambda i: (0, 0)
    return pl.pallas_call(
        _mla_prep_kernel,
        grid=(T // tm,),
        in_specs=[pl.BlockSpec((tm, D), lambda i: (i, 0)),
                  pl.BlockSpec((D, nlat), const),
                  pl.BlockSpec(wq.shape, const),
                  pl.BlockSpec(wkv.shape, const),
                  pl.BlockSpec((1, Q_LORA), const),
                  pl.BlockSpec((1, KV_LORA), const),
                  pl.BlockSpec((1, 2 * LANES), const),
                  pl.BlockSpec((1, 2 * LANES), const),
                  pl.BlockSpec((tm, LANES), lambda i: (i % nS, 0)),
                  pl.BlockSpec((tm, LANES), lambda i: (i % nS, 0))],
        out_specs=[pl.BlockSpec((tm, hq), lambda i: (i, 0)),
                   pl.BlockSpec((tm, hq), lambda i: (i, 0)),
                   pl.BlockSpec((tm, MLA_HEADS * MLA_V), lambda i: (i, 0))],
        out_shape=[jax.ShapeDtypeStruct((T, hq), BF16),
                   jax.ShapeDtypeStruct((T, hq), BF16),
                   jax.ShapeDtypeStruct((T, MLA_HEADS * MLA_V), BF16)],
        compiler_params=_cparams(("parallel",)),
        name="mla_prep",
    )(h2, wlat, wq, wkv, gqa, gkva, gq, gk, cos, sin)


QKV_KINDS = (0, 0, 0, 0, 0, 2, 1, 1, 1, 1, 1, 2)


def _qkv_kernel(h_ref, w_ref, gain_ref, cos_ref, sin_ref, o_ref):
    j = pl.program_id(1)
    acc = jnp.dot(h_ref[...], w_ref[...], preferred_element_type=F32)

    def normed(e):
        x = acc[:, LANES * e: LANES * (e + 1)]
        r = lax.rsqrt(jnp.mean(x * x, axis=-1, keepdims=True) + EPS)
        return x * r * gain_ref[:, LANES * e: LANES * (e + 1)]

    rope_tiles = [t for t, kd in enumerate(QKV_KINDS) if kd == 0]
    norm_tiles = [t for t, kd in enumerate(QKV_KINDS) if kd == 1]

    def is_in(tiles):
        c = j == tiles[0]
        for t in tiles[1:]:
            c = jnp.logical_or(c, j == t)
        return c

    @pl.when(is_in(rope_tiles))
    def _():
        for e in range(2):
            o_ref[:, LANES * e: LANES * (e + 1)] = _rope128(normed(e), cos_ref[...], sin_ref[...]).astype(BF16)

    @pl.when(is_in(norm_tiles))
    def _():
        for e in range(2):
            o_ref[:, LANES * e: LANES * (e + 1)] = normed(e).astype(BF16)

    @pl.when(jnp.logical_not(jnp.logical_or(is_in(rope_tiles), is_in(norm_tiles))))
    def _():
        o_ref[...] = acc.astype(BF16)


def _qkv_proj(h2, w, gains, cos, sin, S):
    T, D = h2.shape
    N = w.shape[1]
    tn = 2 * LANES
    tm = _pick(S, 1024)
    nS = S // tm
    return pl.pallas_call(
        _qkv_kernel,
        grid=(T // tm, N // tn),
        in_specs=[pl.BlockSpec((tm, D), lambda i, j: (i, 0)),
                  pl.BlockSpec((D, tn), lambda i, j: (0, j)),
                  pl.BlockSpec((1, tn), lambda i, j: (0, j)),
                  pl.BlockSpec((tm, LANES), lambda i, j: (i % nS, 0)),
                  pl.BlockSpec((tm, LANES), lambda i, j: (i % nS, 0))],
        out_specs=pl.BlockSpec((tm, tn), lambda i, j: (i, j)),
        out_shape=jax.ShapeDtypeStruct((T, N), BF16),
        compiler_params=_cparams(("parallel", "arbitrary")),
        name="qkv_proj",
    )(h2, w, gains, cos, sin)


def _gate_kernel(h_ref, w_ref, o_ref):
    o_ref[...] = jnp.dot(h_ref[...], w_ref[...], preferred_element_type=F32)


def _gate_proj(h2, w):
    T, D = h2.shape
    N = w.shape[1]
    tm = _pick(T, 2048)
    tn = _pick(N, 512)
    return pl.pallas_call(
        _gate_kernel,
        grid=(T // tm, N // tn),
        in_specs=[pl.BlockSpec((tm, D), lambda i, j: (i, 0)),
                  pl.BlockSpec((D, tn), lambda i, j: (0, j))],
        out_specs=pl.BlockSpec((tm, tn), lambda i, j: (i, j)),
        out_shape=jax.ShapeDtypeStruct((T, N), F32),
        compiler_params=_cparams(("parallel", "arbitrary")),
        name="gate_proj",
    )(h2, w)


def _flash_kernel(q_ref, k_ref, v_ref, g_ref, o_ref, vext_ref, m_ref, acc_ref, *, G, dk, tq, tk, S):
    qi = pl.program_id(2)

    @pl.when(qi == 0)
    def _():
        lane = lax.broadcasted_iota(jnp.int32, (S, LANES), 1)
        vext_ref[:, :LANES] = v_ref[0]
        vext_ref[:, LANES:] = jnp.where(lane == 0, 1.0, 0.0).astype(BF16)

    if G == 1:
        q = q_ref[0]
    else:
        q = jnp.concatenate([q_ref[0, :, dk * g: dk * (g + 1)] for g in range(G)], axis=0)
    m_ref[...] = jnp.full(m_ref.shape, NEG, F32)
    acc_ref[...] = jnp.zeros(acc_ref.shape, F32)

    def body(c, carry):
        off = pl.multiple_of(c * tk, tk)
        k = k_ref[0, pl.ds(off, tk), :]
        s = lax.dot_general(q, k, (((1,), (1,)), ((), ())), preferred_element_type=F32)
        m_old = m_ref[...]
        m_new = jnp.maximum(m_old, jnp.max(s, axis=1, keepdims=True))
        p = jnp.exp2(s - m_new).astype(BF16)
        pv = jnp.dot(p, vext_ref[pl.ds(off, tk), :], preferred_element_type=F32)
        acc_ref[...] = jnp.exp2(m_old - m_new) * acc_ref[...] + pv
        m_ref[...] = m_new
        return carry

    lax.fori_loop(0, S // tk, body, 0)
    acc = acc_ref[...]
    o = acc[:, :LANES] / acc[:, LANES:LANES + 1]
    for g in range(G):
        gate = g_ref[0, :, LANES * g: LANES * (g + 1)]
        o_ref[0, :, LANES * g: LANES * (g + 1)] = (o[tq * g: tq * (g + 1)] * (gate * jax.nn.sigmoid(gate))).astype(BF16)


def _flash_attn(q_arr, q_blk0, k_arr, k_blk0, v_arr, v_blk0, g_arr, g_blk0, *, B, S, H, KV, dk, rows):
    G = H // KV
    tq = rows // G
    tk = _pick(S, 512)
    kern = functools.partial(_flash_kernel, G=G, dk=dk, tq=tq, tk=tk, S=S)
    return pl.pallas_call(
        kern,
        grid=(B, KV, S // tq),
        in_specs=[pl.BlockSpec((1, tq, G * dk), lambda b, h, i: (b, i, q_blk0 + h)),
                  pl.BlockSpec((1, S, dk), lambda b, h, i: (b, 0, k_blk0 + h)),
                  pl.BlockSpec((1, S, LANES), lambda b, h, i: (b, 0, v_blk0 + h)),
                  pl.BlockSpec((1, tq, G * LANES), lambda b, h, i: (b, i, g_blk0 + h))],
        out_specs=pl.BlockSpec((1, tq, G * LANES), lambda b, h, i: (b, i, h)),
        out_shape=jax.ShapeDtypeStruct((B, S, H * LANES), BF16),
        scratch_shapes=[pltpu.VMEM((S, 2 * LANES), BF16),
                        pltpu.VMEM((rows, 1), F32),
                        pltpu.VMEM((rows, 2 * LANES), F32)],
        compiler_params=_cparams(("parallel", "parallel", "arbitrary")),
        name=f"flash_attn_g{G}",
    )(q_arr, k_arr, v_arr, g_arr)


def _win_kernel(q_ref, k0_ref, k1_ref, k2_ref, v0_ref, v1_ref, v2_ref, bias_ref, sink_ref, g_ref, o_ref, *, G, nb):
    j = pl.program_id(2)
    q = jnp.concatenate([q_ref[0, :, LANES * g: LANES * (g + 1)] for g in range(G)], axis=0)
    nt = (((1,), (1,)), ((), ()))
    s0 = lax.dot_general(q, k0_ref[0], nt, preferred_element_type=F32) + bias_ref[0, :, :BLOCK]
    s1 = lax.dot_general(q, k1_ref[0], nt, preferred_element_type=F32) + bias_ref[0, :, BLOCK:2 * BLOCK]
    s2 = lax.dot_general(q, k2_ref[0], nt, preferred_element_type=F32) + bias_ref[0, :, 2 * BLOCK:]
    s0 = s0 + jnp.where(j == 0, NEG, 0.0)
    s2 = s2 + jnp.where(j == nb - 1, NEG, 0.0)
    sink = sink_ref[0]
    m = jnp.maximum(jnp.maximum(jnp.max(s0, axis=1, keepdims=True), jnp.max(s1, axis=1, keepdims=True)),
                    jnp.maximum(jnp.max(s2, axis=1, keepdims=True), sink))
    p0 = jnp.exp2(s0 - m)
    p1 = jnp.exp2(s1 - m)
    p2 = jnp.exp2(s2 - m)
    l = (jnp.sum(p0, axis=1, keepdims=True) + jnp.sum(p1, axis=1, keepdims=True)
         + jnp.sum(p2, axis=1, keepdims=True) + jnp.exp2(sink - m))
    o = (jnp.dot(p0.astype(BF16), v0_ref[0], preferred_element_type=F32)
         + jnp.dot(p1.astype(BF16), v1_ref[0], preferred_element_type=F32)
         + jnp.dot(p2.astype(BF16), v2_ref[0], preferred_element_type=F32)) / l
    for g in range(G):
        gate = g_ref[0, :, LANES * g: LANES * (g + 1)]
        o_ref[0, :, LANES * g: LANES * (g + 1)] = (o[BLOCK * g: BLOCK * (g + 1)] * (gate * jax.nn.sigmoid(gate))).astype(BF16)


def _window_attn(qkv, q_blk0, k_blk0, v_blk0, gate, g_blk0, bias, sink, *, B, S):
    G = WIN_HEADS // WIN_KV
    nb = S // BLOCK
    kern = functools.partial(_win_kernel, G=G, nb=nb)

    def kv_spec(blk0, d):
        return pl.BlockSpec((1, BLOCK, LANES),
                            lambda b, h, j: (b, jnp.clip(j + d, 0, nb - 1), blk0 + h))

    return pl.pallas_call(
        kern,
        grid=(B, WIN_KV, nb),
        in_specs=[pl.BlockSpec((1, BLOCK, G * LANES), lambda b, h, j: (b, j, q_blk0 + h)),
                  kv_spec(k_blk0, -1), kv_spec(k_blk0, 0), kv_spec(k_blk0, 1),
                  kv_spec(v_blk0, -1), kv_spec(v_blk0, 0), kv_spec(v_blk0, 1),
                  pl.BlockSpec((1, G * BLOCK, 3 * BLOCK), lambda b, h, j: (h, 0, 0)),
                  pl.BlockSpec((1, G * BLOCK, 1), lambda b, h, j: (h, 0, 0)),
                  pl.BlockSpec((1, BLOCK, G * LANES), lambda b, h, j: (b, j, g_blk0 + h))],
        out_specs=pl.BlockSpec((1, BLOCK, G * LANES), lambda b, h, j: (b, j, h)),
        out_shape=jax.ShapeDtypeStruct((B, S, WIN_HEADS * LANES), BF16),
        compiler_params=_cparams(("parallel", "parallel", "arbitrary")),
        name="window_attn",
    )(qkv, qkv, qkv, qkv, qkv, qkv, qkv, bias, sink, gate)


def _t5_bucket(rel):
    half = N_BUCKETS // 2
    max_exact = half // 2
    ret = jnp.where(rel > 0, half, 0)
    n = jnp.abs(rel)
    nf = jnp.maximum(n, 1).astype(F32)
    large = max_exact + (jnp.log(nf / max_exact) / math.log(MAX_DIST / max_exact)
                         * (half - max_exact)).astype(jnp.int32)
    large = jnp.minimum(large, half - 1)
    return ret + jnp.where(n < max_exact, n, large)


def _window_bias_table(rel_bias):
    span = BLOCK + 2 * WINDOW
    rel = jnp.arange(span)[None, :] - WINDOW - jnp.arange(BLOCK)[:, None]
    bias = rel_bias[_t5_bucket(rel)].astype(F32) * LOG2E
    bias = jnp.where((jnp.abs(rel) <= WINDOW)[:, :, None], bias, NEG)
    G = WIN_HEADS // WIN_KV
    return bias.transpose(2, 0, 1).reshape(WIN_KV, G * BLOCK, span)


def _branch_kernel(a1, a2, a3, w1, w2, w3, m1, m2, m3, o_ref):
    y = jax.nn.sigmoid(m1[...]) * jnp.dot(a1[...], w1[...], preferred_element_type=F32)
    y = y + jax.nn.sigmoid(m2[...]) * jnp.dot(a2[...], w2[...], preferred_element_type=F32)
    y = y + jax.nn.sigmoid(m3[...]) * jnp.dot(a3[...], w3[...], preferred_element_type=F32)
    o_ref[...] = y.astype(BF16)


def _branch_merge(a_list, w_list, gate, m_col0):
    T, W = a_list[0].shape
    D = w_list[0].shape[1]
    tm = _pick(T, 512)
    tn = _pick(D, 1024)
    nj = D // tn
    a_spec = pl.BlockSpec((tm, W), lambda j, i: (i, 0))
    w_spec = pl.BlockSpec((W, tn), lambda j, i: (0, j))

    def m_spec(k):
        base = (m_col0 + k * D) // tn
        return pl.BlockSpec((tm, tn), lambda j, i: (i, base + j))

    return pl.pallas_call(
        _branch_kernel,
        grid=(nj, T // tm),
        in_specs=[a_spec] * 3 + [w_spec] * 3 + [m_spec(0), m_spec(1), m_spec(2)],
        out_specs=pl.BlockSpec((tm, tn), lambda j, i: (i, j)),
        out_shape=jax.ShapeDtypeStruct((T, D), BF16),
        compiler_params=_cparams(("parallel", "arbitrary")),
        name="branch_merge",
    )(*a_list, *w_list, gate, gate, gate)


def _out_kernel(y_ref, w_ref, x_ref, gt_ref, o_ref):
    o_ref[0] = x_ref[0] + gt_ref[0] * jnp.dot(y_ref[0], w_ref[...], preferred_element_type=F32)


def _out_proj(y, w_out, x, mod):
    B, S, D = x.shape
    tm = _pick(S, 512)
    return pl.pallas_call(
        _out_kernel,
        grid=(B, S // tm),
        in_specs=[pl.BlockSpec((1, tm, D), lambda b, i: (b, i, 0)),
                  pl.BlockSpec((D, D), lambda b, i: (0, 0)),
                  pl.BlockSpec((1, tm, D), lambda b, i: (b, i, 0)),
                  pl.BlockSpec((1, 1, D), lambda b, i: (b, 0, 2))],
        out_specs=pl.BlockSpec((1, tm, D), lambda b, i: (b, i, 0)),
        out_shape=jax.ShapeDtypeStruct((B, S, D), F32),
        compiler_params=_cparams(("parallel", "parallel")),
        name="out_proj",
    )(y, w_out, x, mod)


def _split_points(D):
    sizes = [Q_LORA, KV_LORA, MLA_ROPE, MLA_HEADS * MLA_V,
             AX_HEADS * AX_HD, AX_KV * AX_HD, AX_KV * AX_HD, AX_HEADS * AX_HD,
             WIN_HEADS * WIN_HD, WIN_KV * WIN_HD, WIN_KV * WIN_HD, WIN_HEADS * WIN_HD,
             D, D, D]
    pts, acc = [0], 0
    for s in sizes:
        acc += s
        pts.append(acc)
    return pts


def _prep_weights(D, w_in, w_q_up, w_kv_up, q_a_norm, kv_a_norm, mla_qn, mla_kn,
                  ax_qn, ax_kn, win_qn, win_kn, w_br_mla, w_br_ax, w_br_win, w_out):
    p = _split_points(D)
    col = lambda k: w_in[:, :, p[k]:p[k + 1]]
    (cq, ckv, kr, g_mla, aq, ak, av, g_ax, wq, wk, wv, g_win, m_mla, m_ax, m_win) = [col(k) for k in range(15)]
    L = w_in.shape[0]
    w = {}
    w["lat"] = jnp.concatenate([cq, ckv, kr, kr], axis=-1).astype(BF16)
    w["qkv"] = jnp.concatenate([aq, ak, av, wq, wk, wv], axis=-1).astype(BF16)
    w["gate"] = jnp.concatenate([g_mla, g_ax, g_win, m_mla, m_ax, m_win], axis=-1).astype(BF16)
    wq4 = w_q_up.reshape(L, Q_LORA, MLA_HEADS, MLA_QK)
    w["q_up"] = jnp.concatenate([wq4[..., :MLA_NOPE].reshape(L, Q_LORA, -1),
                                 wq4[..., MLA_NOPE:].reshape(L, Q_LORA, -1)], axis=-1).astype(BF16)
    wkv4 = w_kv_up.reshape(L, KV_LORA, MLA_HEADS, MLA_NOPE + MLA_V)
    w["kv_up"] = jnp.concatenate([wkv4[..., :MLA_NOPE].reshape(L, KV_LORA, -1),
                                  wkv4[..., MLA_NOPE:].reshape(L, KV_LORA, -1)], axis=-1).astype(BF16)
    w["gqa"] = q_a_norm.reshape(L, 1, Q_LORA)
    w["gkva"] = kv_a_norm.reshape(L, 1, KV_LORA)
    qs_mla = MLA_QK ** -0.5 * LOG2E
    w["gq"] = (jnp.concatenate([mla_qn, mla_qn[:, MLA_NOPE:]], axis=-1) * qs_mla).reshape(L, 1, 2 * LANES)
    w["gk"] = jnp.concatenate([mla_kn, mla_kn[:, MLA_NOPE:]], axis=-1).reshape(L, 1, 2 * LANES)
    ones = jnp.ones((L, AX_KV * AX_HD), F32)
    w["qkv_gain"] = jnp.concatenate(
        [jnp.tile(ax_qn, (1, AX_HEADS)) * (AX_HD ** -0.5 * LOG2E), jnp.tile(ax_kn, (1, AX_KV)), ones,
         jnp.tile(win_qn, (1, WIN_HEADS)) * (WIN_HD ** -0.5 * LOG2E), jnp.tile(win_kn, (1, WIN_KV)), ones],
        axis=-1).reshape(L, 1, -1)
    w["br"] = [w_br_mla.astype(BF16), w_br_ax.astype(BF16), w_br_win.astype(BF16)]
    w["out"] = w_out.astype(BF16)
    return w


def _layer(x, mod, l, ln_g, w, tabs, bias_tab, sink_tab):
    B, S, D = x.shape
    T = B * S
    h = _norm_modulate(x, ln_g[l], mod)
    h2 = h.reshape(T, D)
    cos_m, sin_m, cos_a, sin_a = tabs[S]
    q_m, k_m, v_m = _mla_prep(h2, w["lat"][l], w["q_up"][l], w["kv_up"][l], w["gqa"][l], w["gkva"][l],
                              w["gq"][l], w["gk"][l], cos_m, sin_m, S)
    qkv = _qkv_proj(h2, w["qkv"][l], w["qkv_gain"][l], cos_a, sin_a, S)
    gate = _gate_proj(h2, w["gate"][l])
    r3 = lambda a: a.reshape(B, S, a.shape[-1])
    q_m, k_m, v_m, qkv3, gate3 = r3(q_m), r3(k_m), r3(v_m), r3(qkv), r3(gate)
    a_mla = _flash_attn(q_m, 0, k_m, 0, v_m, 0, gate3, 0,
                        B=B, S=S, H=MLA_HEADS, KV=MLA_HEADS, dk=2 * LANES, rows=min(512, S))
    gq = AX_HEADS // AX_KV
    a_ax = _flash_attn(qkv3, 0, qkv3, AX_HEADS, qkv3, AX_HEADS + AX_KV, gate3, AX_KV,
                       B=B, S=S, H=AX_HEADS, KV=AX_KV, dk=AX_HD, rows=min(512, S))
    wbase = AX_HEADS + 2 * AX_KV
    a_win = _window_attn(qkv3, wbase // gq, wbase + WIN_HEADS, wbase + WIN_HEADS + WIN_KV,
                         gate3, 2 * AX_KV, bias_tab, sink_tab[l], B=B, S=S)
    a_list = [a.reshape(T, a.shape[-1]) for a in (a_mla, a_ax, a_win)]
    y = _branch_merge(a_list, [wb[l] for wb in w["br"]], gate, 3 * MLA_HEADS * MLA_V)
    return _out_proj(y.reshape(B, S, D), w["out"][l], x, mod)


def kernel(x_prompt, x_sample, c_prompt, c_sample, ln_g, w_ada, b_ada, w_in, q_a_norm, w_q_up, kv_a_norm, w_kv_up, mla_qn, mla_kn, ax_qn, ax_kn, win_qn, win_kn, win_sink, rel_bias, w_br_mla, w_br_ax, w_br_win, w_out):
    L = ln_g.shape[0]
    D = x_prompt.shape[-1]
    Bp = x_prompt.shape[0]
    w = _prep_weights(D, w_in, w_q_up, w_kv_up, q_a_norm, kv_a_norm, mla_qn, mla_kn,
                      ax_qn, ax_kn, win_qn, win_kn, w_br_mla, w_br_ax, w_br_win, w_out)
    mod_all = _modulation(jnp.concatenate([c_prompt, c_sample], axis=0), w_ada, b_ada)
    tabs = {}
    for S in {x_prompt.shape[1], x_sample.shape[1]}:
        pos = jnp.arange(S, dtype=F32)
        row = jnp.repeat(jnp.arange(S // GRID_W, dtype=F32), GRID_W)
        colp = jnp.tile(jnp.arange(GRID_W, dtype=F32), S // GRID_W)
        tabs[S] = _rope_tables(pos, pos) + _rope_tables(row, colp)
    bias_tab = _window_bias_table(rel_bias)
    G = WIN_HEADS // WIN_KV
    sink_tab = jnp.repeat(win_sink.astype(F32) * LOG2E, BLOCK, axis=-1).reshape(L, WIN_KV, G * BLOCK, 1)
    yp, ys = x_prompt, x_sample
    for l in range(L):
        mod_p = mod_all[l, :Bp][:, None, :]
        mod_s = mod_all[l, Bp:][:, None, :]
        yp = _layer(yp, mod_p, l, ln_g, w, tabs, bias_tab, sink_tab)
        ys = _layer(ys, mod_s, l, ln_g, w, tabs, bias_tab, sink_tab)
    return (yp, ys)
```

```python
import functools
import math

import jax
import jax.numpy as jnp
from jax import lax
from jax.experimental import pallas as pl
from jax.experimental.pallas import tpu as pltpu

F32 = jnp.float32
BF16 = jnp.bfloat16

GRID_W = 64
BLOCK = 128
EPS = 1e-6
ROPE_THETA = 10000.0
NEG = -1e30
LOG2E = 1.4426950408889634
LANES = 128

MLA_HEADS = 8
MLA_NOPE = 128
MLA_ROPE = 64
MLA_QK = MLA_NOPE + MLA_ROPE
MLA_V = 128
Q_LORA = 512
KV_LORA = 256
AX_HEADS = 8
AX_KV = 2
AX_HD = 128
WIN_HEADS = 8
WIN_KV = 2
WIN_HD = 128
WINDOW = 128
N_BUCKETS = 32
MAX_DIST = 128
ROPE_HALF = 32

VMEM_LIMIT = 56 * 1024 * 1024


def _cparams(sem):
    return pltpu.CompilerParams(dimension_semantics=sem, vmem_limit_bytes=VMEM_LIMIT)


def _pick(n, pref):
    t = min(pref, n)
    while n % t:
        t //= 2
    return t


def _mod_kernel(c_ref, w_ref, b_ref, o_ref):
    o_ref[0] = jnp.dot(c_ref[...], w_ref[0].astype(BF16), preferred_element_type=F32) + b_ref[0]


def _modulation(c_all, w_ada, b_ada):
    L, D, N = w_ada.shape
    R = c_all.shape[0]
    tn = _pick(N, 512)
    return pl.pallas_call(
        _mod_kernel,
        grid=(L, N // tn),
        in_specs=[pl.BlockSpec((R, D), lambda l, j: (0, 0)),
                  pl.BlockSpec((1, D, tn), lambda l, j: (l, 0, j)),
                  pl.BlockSpec((1, 1, tn), lambda l, j: (l, 0, j))],
        out_specs=pl.BlockSpec((1, R, tn), lambda l, j: (l, 0, j)),
        out_shape=jax.ShapeDtypeStruct((L, R, N), F32),
        compiler_params=_cparams(("parallel", "parallel")),
        name="adaln_mod",
    )(c_all.astype(BF16), w_ada, b_ada.reshape(L, 1, N))


def _h_kernel(x_ref, g_ref, sh_ref, sc_ref, o_ref):
    x = x_ref[0]
    r = lax.rsqrt(jnp.mean(x * x, axis=-1, keepdims=True) + EPS)
    y = x * r * g_ref[...]
    o_ref[0] = (y * (1.0 + sc_ref[0]) + sh_ref[0]).astype(BF16)


def _norm_modulate(x, ln_g, mod):
    B, S, D = x.shape
    ts = _pick(S, 512)
    return pl.pallas_call(
        _h_kernel,
        grid=(B, S // ts),
        in_specs=[pl.BlockSpec((1, ts, D), lambda b, i: (b, i, 0)),
                  pl.BlockSpec((1, D), lambda b, i: (0, 0)),
                  pl.BlockSpec((1, 1, D), lambda b, i: (b, 0, 0)),
                  pl.BlockSpec((1, 1, D), lambda b, i: (b, 0, 1))],
        out_specs=pl.BlockSpec((1, ts, D), lambda b, i: (b, i, 0)),
        out_shape=jax.ShapeDtypeStruct((B, S, D), BF16),
        compiler_params=_cparams(("parallel", "parallel")),
        name="norm_modulate",
    )(x, ln_g.reshape(1, D), mod, mod)


def _swap32(y):
    lane = lax.broadcasted_iota(jnp.int32, y.shape, 1)
    return jnp.where((lane % (2 * ROPE_HALF)) < ROPE_HALF,
                     pltpu.roll(y, LANES - ROPE_HALF, 1), pltpu.roll(y, ROPE_HALF, 1))


def _rope128(y, cos, sin_signed):
    return y * cos + _swap32(y) * sin_signed


def _rope_tables(pos_a, pos_b):
    freqs = ROPE_THETA ** (-jnp.arange(ROPE_HALF, dtype=F32) / ROPE_HALF)

    def seg(pos):
        ang = pos[:, None] * freqs[None, :]
        c, s = jnp.cos(ang), jnp.sin(ang)
        return jnp.concatenate([c, c], axis=-1), jnp.concatenate([-s, s], axis=-1)

    ca, sa = seg(pos_a)
    cb, sb = seg(pos_b)
    return jnp.concatenate([ca, cb], axis=-1), jnp.concatenate([sa, sb], axis=-1)


def _mla_prep_kernel(h_ref, wlat_ref, wq_ref, wkv_ref, gqa_ref, gkva_ref, gq_ref, gk_ref,
                     cos_ref, sin_ref, q_ref, k_ref, v_ref, *, rc):
    for r in range(h_ref.shape[0] // rc):
        _mla_prep_rows(slice(rc * r, rc * (r + 1)), h_ref, wlat_ref, wq_ref, wkv_ref, gqa_ref, gkva_ref,
                       gq_ref, gk_ref, cos_ref, sin_ref, q_ref, k_ref, v_ref)


def _mla_prep_rows(rows, h_ref, wlat_ref, wq_ref, wkv_ref, gqa_ref, gkva_ref, gq_ref, gk_ref,
                   cos_ref, sin_ref, q_ref, k_ref, v_ref):
    lat = jnp.dot(h_ref[rows, :], wlat_ref[...], preferred_element_type=F32)
    cq = lat[:, :Q_LORA]
    ckv = lat[:, Q_LORA:Q_LORA + KV_LORA]
    kr2 = lat[:, Q_LORA + KV_LORA:]
    cqn = cq * lax.rsqrt(jnp.mean(cq * cq, axis=-1, keepdims=True) + EPS) * gqa_ref[...]
    ckvn = ckv * lax.rsqrt(jnp.mean(ckv * ckv, axis=-1, keepdims=True) + EPS) * gkva_ref[...]
    qraw = jnp.dot(cqn.astype(BF16), wq_ref[...], preferred_element_type=F32)
    kvraw = jnp.dot(ckvn.astype(BF16), wkv_ref[...], preferred_element_type=F32)
    cos = cos_ref[rows, :]
    sin = sin_ref[rows, :]
    lane = lax.broadcasted_iota(jnp.int32, cos.shape, 1)
    lo = lane < MLA_ROPE
    gq = gq_ref[...]
    gk = gk_ref[...]
    nope_w = MLA_HEADS * MLA_NOPE
    inv_qk = 1.0 / MLA_QK
    kr_sq = kr2 * kr2
    ss_kr = jnp.sum(jnp.where(lo, kr_sq, 0.0), axis=-1, keepdims=True)
    k_rope = _rope128(kr2 * gk[:, LANES:], cos, sin)
    for p in range(MLA_HEADS // 2):
        qr = qraw[:, nope_w + LANES * p: nope_w + LANES * (p + 1)]
        qr_sq = qr * qr
        ss_r = (jnp.sum(jnp.where(lo, qr_sq, 0.0), axis=-1, keepdims=True),
                jnp.sum(jnp.where(lo, 0.0, qr_sq), axis=-1, keepdims=True))
        q_rope = _rope128(qr * gq[:, LANES:], cos, sin)
        for e in range(2):
            hd = 2 * p + e
            keep = lo if e == 0 else jnp.logical_not(lo)
            qn = qraw[:, LANES * hd: LANES * (hd + 1)]
            rq = lax.rsqrt((jnp.sum(qn * qn, axis=-1, keepdims=True) + ss_r[e]) * inv_qk + EPS)
            q_ref[rows, 2 * LANES * hd: 2 * LANES * hd + LANES] = (qn * rq * gq[:, :LANES]).astype(BF16)
            q_ref[rows, 2 * LANES * hd + LANES: 2 * LANES * (hd + 1)] = jnp.where(keep, q_rope * rq, 0.0).astype(BF16)
            kn = kvraw[:, LANES * hd: LANES * (hd + 1)]
            rk = lax.rsqrt((jnp.sum(kn * kn, axis=-1, keepdims=True) + ss_kr) * inv_qk + EPS)
            k_ref[rows, 2 * LANES * hd: 2 * LANES * hd + LANES] = (kn * rk * gk[:, :LANES]).astype(BF16)
            k_ref[rows, 2 * LANES * hd + LANES: 2 * LANES * (hd + 1)] = jnp.where(keep, k_rope * rk, 0.0).astype(BF16)
    v_ref[rows, :] = kvraw[:, nope_w:].astype(BF16)


def _mla_prep(h2, wlat, wq, wkv, gqa, gkva, gq, gk, cos, sin, S):
    T, D = h2.shape
    tm = _pick(S, 512)
    nS = S // tm
    nlat = wlat.shape[1]
    hq = MLA_HEADS * 2 * LANES
    const = lambda i: (0, 0)
    return pl.pallas_call(
        functools.partial(_mla_prep_kernel, rc=min(128, tm)),
        grid=(T // tm,),
        in_specs=[pl.BlockSpec((tm, D), lambda i: (i, 0)),
                  pl.BlockSpec((D, nlat), const),
                  pl.BlockSpec(wq.shape, const),
                  pl.BlockSpec(wkv.shape, const),
                  pl.BlockSpec((1, Q_LORA), const),
                  pl.BlockSpec((1, KV_LORA), const),
                  pl.BlockSpec((1, 2 * LANES), const),
                  pl.BlockSpec((1, 2 * LANES), const),
                  pl.BlockSpec((tm, LANES), lambda i: (i % nS, 0)),
                  pl.BlockSpec((tm, LANES), lambda i: (i % nS, 0))],
        out_specs=[pl.BlockSpec((tm, hq), lambda i: (i, 0)),
                   pl.BlockSpec((tm, hq), lambda i: (i, 0)),
                   pl.BlockSpec((tm, MLA_HEADS * MLA_V), lambda i: (i, 0))],
        out_shape=[jax.ShapeDtypeStruct((T, hq), BF16),
                   jax.ShapeDtypeStruct((T, hq), BF16),
                   jax.ShapeDtypeStruct((T, MLA_HEADS * MLA_V), BF16)],
        compiler_params=_cparams(("parallel",)),
        name="mla_prep",
    )(h2, wlat, wq, wkv, gqa, gkva, gq, gk, cos, sin)


QKV_KINDS = (0, 0, 0, 0, 0, 2, 1, 1, 1, 1, 1, 2)


def _qkv_kernel(h_ref, w_ref, gain_ref, cos_ref, sin_ref, o_ref, *, rc):
    j = pl.program_id(1)
    tm = h_ref.shape[0]

    def run(kind):
        for r in range(tm // rc):
            rows = slice(rc * r, rc * (r + 1))
            acc = jnp.dot(h_ref[rows, :], w_ref[...], preferred_element_type=F32)
            if kind == 2:
                o_ref[rows, :] = acc.astype(BF16)
                continue
            for e in range(2):
                lanes = slice(LANES * e, LANES * (e + 1))
                x = acc[:, lanes]
                y = x * lax.rsqrt(jnp.mean(x * x, axis=-1, keepdims=True) + EPS) * gain_ref[:, lanes]
                if kind == 0:
                    y = _rope128(y, cos_ref[rows, :], sin_ref[rows, :])
                o_ref[rows, lanes] = y.astype(BF16)

    def is_kind(kind):
        c = None
        for t, kd in enumerate(QKV_KINDS):
            if kd == kind:
                c = (j == t) if c is None else jnp.logical_or(c, j == t)
        return c

    for kind in range(3):
        pl.when(is_kind(kind))(functools.partial(run, kind))


def _qkv_proj(h2, w, gains, cos, sin, S):
    T, D = h2.shape
    N = w.shape[1]
    tn = 2 * LANES
    tm = _pick(S, 1024)
    nS = S // tm
    return pl.pallas_call(
        functools.partial(_qkv_kernel, rc=min(256, tm)),
        grid=(T // tm, N // tn),
        in_specs=[pl.BlockSpec((tm, D), lambda i, j: (i, 0)),
                  pl.BlockSpec((D, tn), lambda i, j: (0, j)),
                  pl.BlockSpec((1, tn), lambda i, j: (0, j)),
                  pl.BlockSpec((tm, LANES), lambda i, j: (i % nS, 0)),
                  pl.BlockSpec((tm, LANES), lambda i, j: (i % nS, 0))],
        out_specs=pl.BlockSpec((tm, tn), lambda i, j: (i, j)),
        out_shape=jax.ShapeDtypeStruct((T, N), BF16),
        compiler_params=_cparams(("parallel", "arbitrary")),
        name="qkv_proj",
    )(h2, w, gains, cos, sin)


def _gate_kernel(h_ref, w_ref, o_ref):
    o_ref[...] = jnp.dot(h_ref[...], w_ref[...], preferred_element_type=F32)


def _gate_proj(h2, w):
    T, D = h2.shape
    N = w.shape[1]
    tm = _pick(T, 2048)
    tn = _pick(N, 512)
    return pl.pallas_call(
        _gate_kernel,
        grid=(T // tm, N // tn),
        in_specs=[pl.BlockSpec((tm, D), lambda i, j: (i, 0)),
                  pl.BlockSpec((D, tn), lambda i, j: (0, j))],
        out_specs=pl.BlockSpec((tm, tn), lambda i, j: (i, j)),
        out_shape=jax.ShapeDtypeStruct((T, N), F32),
        compiler_params=_cparams(("parallel", "arbitrary")),
        name="gate_proj",
    )(h2, w)


def _flash_kernel(q_ref, k_ref, v_ref, g_ref, o_ref, vext_ref, *, G, dk, tq, tk, S):
    qi = pl.program_id(2)

    @pl.when(qi == 0)
    def _():
        lane = lax.broadcasted_iota(jnp.int32, (S, LANES), 1)
        vext_ref[:, :LANES] = v_ref[0]
        vext_ref[:, LANES:] = jnp.where(lane == 0, 1.0, 0.0).astype(BF16)

    if G == 1:
        q = q_ref[0]
    else:
        q = jnp.concatenate([q_ref[0, :, dk * g: dk * (g + 1)] for g in range(G)], axis=0)
    n = S // tk

    def scores(c):
        k = k_ref[0, tk * c: tk * (c + 1), :]
        return lax.dot_general(q, k, (((1,), (1,)), ((), ())), preferred_element_type=F32)

    s_next = scores(0)
    m = None
    acc = None
    for c in range(n):
        s = s_next
        if c + 1 < n:
            s_next = scores(c + 1)
        m_blk = jnp.max(s, axis=1, keepdims=True)
        m_new = m_blk if m is None else jnp.maximum(m, m_blk)
        p = jnp.exp2(s - m_new).astype(BF16)
        pv = jnp.dot(p, vext_ref[tk * c: tk * (c + 1), :], preferred_element_type=F32)
        acc = pv if acc is None else jnp.exp2(m - m_new) * acc + pv
        m = m_new
    o = acc[:, :LANES] / acc[:, LANES:LANES + 1]
    for g in range(G):
        gate = g_ref[0, :, LANES * g: LANES * (g + 1)]
        o_ref[0, :, LANES * g: LANES * (g + 1)] = (o[tq * g: tq * (g + 1)] * (gate * jax.nn.sigmoid(gate))).astype(BF16)


def _flash_attn(q_arr, q_blk0, k_arr, k_blk0, v_arr, v_blk0, g_arr, g_blk0, *, B, S, H, KV, dk, rows):
    G = H // KV
    tq = rows // G
    tk = _pick(S, 1024)
    kern = functools.partial(_flash_kernel, G=G, dk=dk, tq=tq, tk=tk, S=S)
    return pl.pallas_call(
        kern,
        grid=(B, KV, S // tq),
        in_specs=[pl.BlockSpec((1, tq, G * dk), lambda b, h, i: (b, i, q_blk0 + h)),
                  pl.BlockSpec((1, S, dk), lambda b, h, i: (b, 0, k_blk0 + h)),
                  pl.BlockSpec((1, S, LANES), lambda b, h, i: (b, 0, v_blk0 + h)),
                  pl.BlockSpec((1, tq, G * LANES), lambda b, h, i: (b, i, g_blk0 + h))],
        out_specs=pl.BlockSpec((1, tq, G * LANES), lambda b, h, i: (b, i, h)),
        out_shape=jax.ShapeDtypeStruct((B, S, H * LANES), BF16),
        scratch_shapes=[pltpu.VMEM((S, 2 * LANES), BF16)],
        compiler_params=_cparams(("parallel", "parallel", "arbitrary")),
        name=f"flash_attn_g{G}",
    )(q_arr, k_arr, v_arr, g_arr)


def _win_kernel(q_ref, kp_ref, kc_ref, kn_ref, vp_ref, vc_ref, vn_ref, bias_ref, sink_ref, g_ref, o_ref, *, G, nq, nsteps):
    j = pl.program_id(2)
    nt = (((1,), (1,)), ((), ()))
    sink = sink_ref[0]

    def kv_block(prev_ref, cur_ref, next_ref, t):
        if t < 0:
            return prev_ref[0]
        if t >= nq:
            return next_ref[0]
        return cur_ref[0, BLOCK * t: BLOCK * (t + 1), :]

    for t in range(nq):
        rows = slice(BLOCK * t, BLOCK * (t + 1))
        q = jnp.concatenate([q_ref[0, rows, LANES * g: LANES * (g + 1)] for g in range(G)], axis=0)
        s = [lax.dot_general(q, kv_block(kp_ref, kc_ref, kn_ref, t - 1 + d), nt, preferred_element_type=F32)
             + bias_ref[0, :, BLOCK * d: BLOCK * (d + 1)] for d in range(3)]
        if t == 0:
            s[0] = s[0] + jnp.where(j == 0, NEG, 0.0)
        if t == nq - 1:
            s[2] = s[2] + jnp.where(j == nsteps - 1, NEG, 0.0)
        m = jnp.maximum(jnp.max(jnp.maximum(jnp.maximum(s[0], s[1]), s[2]), axis=1, keepdims=True), sink)
        p = [jnp.exp2(sd - m) for sd in s]
        l = jnp.sum(p[0] + p[1] + p[2], axis=1, keepdims=True) + jnp.exp2(sink - m)
        o = sum(jnp.dot(p[d].astype(BF16), kv_block(vp_ref, vc_ref, vn_ref, t - 1 + d), preferred_element_type=F32)
                for d in range(3)) / l
        for g in range(G):
            gate = g_ref[0, rows, LANES * g: LANES * (g + 1)]
            o_ref[0, rows, LANES * g: LANES * (g + 1)] = (
                o[BLOCK * g: BLOCK * (g + 1)] * (gate * jax.nn.sigmoid(gate))).astype(BF16)


def _window_attn(qkv, q_blk0, k_blk0, v_blk0, gate, g_blk0, bias, sink, *, B, S):
    G = WIN_HEADS // WIN_KV
    nb = S // BLOCK
    nq = 4 if nb % 4 == 0 else 1
    nsteps = nb // nq
    kern = functools.partial(_win_kernel, G=G, nq=nq, nsteps=nsteps)

    def kv_specs(blk0):
        return [pl.BlockSpec((1, BLOCK, LANES), lambda b, h, j: (b, jnp.maximum(j * nq - 1, 0), blk0 + h)),
                pl.BlockSpec((1, nq * BLOCK, LANES), lambda b, h, j: (b, j, blk0 + h)),
                pl.BlockSpec((1, BLOCK, LANES), lambda b, h, j: (b, jnp.minimum((j + 1) * nq, nb - 1), blk0 + h))]

    return pl.pallas_call(
        kern,
        grid=(B, WIN_KV, nsteps),
        in_specs=[pl.BlockSpec((1, nq * BLOCK, G * LANES), lambda b, h, j: (b, j, q_blk0 + h))]
        + kv_specs(k_blk0) + kv_specs(v_blk0)
        + [pl.BlockSpec((1, G * BLOCK, 3 * BLOCK), lambda b, h, j: (h, 0, 0)),
           pl.BlockSpec((1, G * BLOCK, 1), lambda b, h, j: (h, 0, 0)),
           pl.BlockSpec((1, nq * BLOCK, G * LANES), lambda b, h, j: (b, j, g_blk0 + h))],
        out_specs=pl.BlockSpec((1, nq * BLOCK, G * LANES), lambda b, h, j: (b, j, h)),
        out_shape=jax.ShapeDtypeStruct((B, S, WIN_HEADS * LANES), BF16),
        compiler_params=_cparams(("parallel", "parallel", "arbitrary")),
        name="window_attn",
    )(qkv, qkv, qkv, qkv, qkv, qkv, qkv, bias, sink, gate)


def _t5_bucket(rel):
    half = N_BUCKETS // 2
    max_exact = half // 2
    ret = jnp.where(rel > 0, half, 0)
    n = jnp.abs(rel)
    nf = jnp.maximum(n, 1).astype(F32)
    large = max_exact + (jnp.log(nf / max_exact) / math.log(MAX_DIST / max_exact)
                         * (half - max_exact)).astype(jnp.int32)
    large = jnp.minimum(large, half - 1)
    return ret + jnp.where(n < max_exact, n, large)


def _bias_kernel(bucket_ref, rb_ref, o_ref):
    bucket = bucket_ref[...]
    row = lax.broadcasted_iota(jnp.int32, bucket.shape, 0)
    colk = lax.broadcasted_iota(jnp.int32, bucket.shape, 1)
    inband = jnp.abs(colk - WINDOW - row) <= WINDOW
    for h in range(WIN_HEADS):
        acc = jnp.zeros(bucket.shape, F32)
        for b in range(N_BUCKETS):
            acc = jnp.where(bucket == b, rb_ref[b, h], acc)
        o_ref[h] = jnp.where(inband, acc * LOG2E, NEG)


def _window_bias_table(rel_bias):
    span = BLOCK + 2 * WINDOW
    rel = jnp.arange(span)[None, :] - WINDOW - jnp.arange(BLOCK)[:, None]
    bias = pl.pallas_call(
        _bias_kernel,
        in_specs=[pl.BlockSpec(memory_space=pltpu.VMEM), pl.BlockSpec(memory_space=pltpu.SMEM)],
        out_specs=pl.BlockSpec(memory_space=pltpu.VMEM),
        out_shape=jax.ShapeDtypeStruct((WIN_HEADS, BLOCK, span), F32),
        name="t5_bias_table",
    )(_t5_bucket(rel).astype(jnp.int32), rel_bias.astype(F32))
    G = WIN_HEADS // WIN_KV
    return bias.reshape(WIN_KV, G * BLOCK, span)


def _branch_kernel(a1, a2, a3, w1, w2, w3, m1, m2, m3, o_ref):
    y = jax.nn.sigmoid(m1[...]) * jnp.dot(a1[...], w1[...], preferred_element_type=F32)
    y = y + jax.nn.sigmoid(m2[...]) * jnp.dot(a2[...], w2[...], preferred_element_type=F32)
    y = y + jax.nn.sigmoid(m3[...]) * jnp.dot(a3[...], w3[...], preferred_element_type=F32)
    o_ref[...] = y.astype(BF16)


def _branch_merge(a_list, w_list, gate, m_col0):
    T, W = a_list[0].shape
    D = w_list[0].shape[1]
    tm = _pick(T, 512)
    tn = _pick(D, 1024)
    nj = D // tn
    a_spec = pl.BlockSpec((tm, W), lambda j, i: (i, 0))
    w_spec = pl.BlockSpec((W, tn), lambda j, i: (0, j))

    def m_spec(k):
        base = (m_col0 + k * D) // tn
        return pl.BlockSpec((tm, tn), lambda j, i: (i, base + j))

    return pl.pallas_call(
        _branch_kernel,
        grid=(nj, T // tm),
        in_specs=[a_spec] * 3 + [w_spec] * 3 + [m_spec(0), m_spec(1), m_spec(2)],
        out_specs=pl.BlockSpec((tm, tn), lambda j, i: (i, j)),
        out_shape=jax.ShapeDtypeStruct((T, D), BF16),
        compiler_params=_cparams(("parallel", "arbitrary")),
        name="branch_merge",
    )(*a_list, *w_list, gate, gate, gate)


def _out_kernel(y_ref, w_ref, x_ref, gt_ref, o_ref):
    o_ref[0] = x_ref[0] + gt_ref[0] * jnp.dot(y_ref[0], w_ref[...], preferred_element_type=F32)


def _out_proj(y, w_out, x, mod):
    B, S, D = x.shape
    tm = _pick(S, 512)
    return pl.pallas_call(
        _out_kernel,
        grid=(B, S // tm),
        in_specs=[pl.BlockSpec((1, tm, D), lambda b, i: (b, i, 0)),
                  pl.BlockSpec((D, D), lambda b, i: (0, 0)),
                  pl.BlockSpec((1, tm, D), lambda b, i: (b, i, 0)),
                  pl.BlockSpec((1, 1, D), lambda b, i: (b, 0, 2))],
        out_specs=pl.BlockSpec((1, tm, D), lambda b, i: (b, i, 0)),
        out_shape=jax.ShapeDtypeStruct((B, S, D), F32),
        compiler_params=_cparams(("parallel", "parallel")),
        name="out_proj",
    )(y, w_out, x, mod)


def _split_points(D):
    sizes = [Q_LORA, KV_LORA, MLA_ROPE, MLA_HEADS * MLA_V,
             AX_HEADS * AX_HD, AX_KV * AX_HD, AX_KV * AX_HD, AX_HEADS * AX_HD,
             WIN_HEADS * WIN_HD, WIN_KV * WIN_HD, WIN_KV * WIN_HD, WIN_HEADS * WIN_HD,
             D, D, D]
    pts, acc = [0], 0
    for s in sizes:
        acc += s
        pts.append(acc)
    return pts


def _prep_weights(D, w_in, w_q_up, w_kv_up, q_a_norm, kv_a_norm, mla_qn, mla_kn,
                  ax_qn, ax_kn, win_qn, win_kn, w_br_mla, w_br_ax, w_br_win, w_out):
    p = _split_points(D)
    col = lambda k: w_in[:, :, p[k]:p[k + 1]]
    (cq, ckv, kr, g_mla, aq, ak, av, g_ax, wq, wk, wv, g_win, m_mla, m_ax, m_win) = [col(k) for k in range(15)]
    L = w_in.shape[0]
    w = {}
    w["lat"] = jnp.concatenate([cq, ckv, kr, kr], axis=-1).astype(BF16)
    w["qkv"] = jnp.concatenate([aq, ak, av, wq, wk, wv], axis=-1).astype(BF16)
    w["gate"] = jnp.concatenate([g_mla, g_ax, g_win, m_mla, m_ax, m_win], axis=-1).astype(BF16)
    wq4 = w_q_up.reshape(L, Q_LORA, MLA_HEADS, MLA_QK)
    w["q_up"] = jnp.concatenate([wq4[..., :MLA_NOPE].reshape(L, Q_LORA, -1),
                                 wq4[..., MLA_NOPE:].reshape(L, Q_LORA, -1)], axis=-1).astype(BF16)
    wkv4 = w_kv_up.reshape(L, KV_LORA, MLA_HEADS, MLA_NOPE + MLA_V)
    w["kv_up"] = jnp.concatenate([wkv4[..., :MLA_NOPE].reshape(L, KV_LORA, -1),
                                  wkv4[..., MLA_NOPE:].reshape(L, KV_LORA, -1)], axis=-1).astype(BF16)
    w["gqa"] = q_a_norm.reshape(L, 1, Q_LORA)
    w["gkva"] = kv_a_norm.reshape(L, 1, KV_LORA)
    qs_mla = MLA_QK ** -0.5 * LOG2E
    w["gq"] = (jnp.concatenate([mla_qn, mla_qn[:, MLA_NOPE:]], axis=-1) * qs_mla).reshape(L, 1, 2 * LANES)
    w["gk"] = jnp.concatenate([mla_kn, mla_kn[:, MLA_NOPE:]], axis=-1).reshape(L, 1, 2 * LANES)
    ones = jnp.ones((L, AX_KV * AX_HD), F32)
    w["qkv_gain"] = jnp.concatenate(
        [jnp.tile(ax_qn, (1, AX_HEADS)) * (AX_HD ** -0.5 * LOG2E), jnp.tile(ax_kn, (1, AX_KV)), ones,
         jnp.tile(win_qn, (1, WIN_HEADS)) * (WIN_HD ** -0.5 * LOG2E), jnp.tile(win_kn, (1, WIN_KV)), ones],
        axis=-1).reshape(L, 1, -1)
    w["br"] = [w_br_mla.astype(BF16), w_br_ax.astype(BF16), w_br_win.astype(BF16)]
    w["out"] = w_out.astype(BF16)
    return w


def _layer(x, mod, l, ln_g, w, tabs, bias_tab, sink_tab):
    B, S, D = x.shape
    T = B * S
    h = _norm_modulate(x, ln_g[l], mod)
    h2 = h.reshape(T, D)
    cos_m, sin_m, cos_a, sin_a = tabs[S]
    q_m, k_m, v_m = _mla_prep(h2, w["lat"][l], w["q_up"][l], w["kv_up"][l], w["gqa"][l], w["gkva"][l],
                              w["gq"][l], w["gk"][l], cos_m, sin_m, S)
    qkv = _qkv_proj(h2, w["qkv"][l], w["qkv_gain"][l], cos_a, sin_a, S)
    gate = _gate_proj(h2, w["gate"][l])
    r3 = lambda a: a.reshape(B, S, a.shape[-1])
    q_m, k_m, v_m, qkv3, gate3 = r3(q_m), r3(k_m), r3(v_m), r3(qkv), r3(gate)
    a_mla = _flash_attn(q_m, 0, k_m, 0, v_m, 0, gate3, 0,
                        B=B, S=S, H=MLA_HEADS, KV=MLA_HEADS, dk=2 * LANES, rows=min(1024, S))
    gq = AX_HEADS // AX_KV
    a_ax = _flash_attn(qkv3, 0, qkv3, AX_HEADS, qkv3, AX_HEADS + AX_KV, gate3, AX_KV,
                       B=B, S=S, H=AX_HEADS, KV=AX_KV, dk=AX_HD, rows=min(1024, S))
    wbase = AX_HEADS + 2 * AX_KV
    a_win = _window_attn(qkv3, wbase // gq, wbase + WIN_HEADS, wbase + WIN_HEADS + WIN_KV,
                         gate3, 2 * AX_KV, bias_tab, sink_tab[l], B=B, S=S)
    a_list = [a.reshape(T, a.shape[-1]) for a in (a_mla, a_ax, a_win)]
    y = _branch_merge(a_list, [wb[l] for wb in w["br"]], gate, 3 * MLA_HEADS * MLA_V)
    return _out_proj(y.reshape(B, S, D), w["out"][l], x, mod)


def kernel(x_prompt, x_sample, c_prompt, c_sample, ln_g, w_ada, b_ada, w_in, q_a_norm, w_q_up, kv_a_norm, w_kv_up, mla_qn, mla_kn, ax_qn, ax_kn, win_qn, win_kn, win_sink, rel_bias, w_br_mla, w_br_ax, w_br_win, w_out):
    L = ln_g.shape[0]
    D = x_prompt.shape[-1]
    Bp = x_prompt.shape[0]
    w = _prep_weights(D, w_in, w_q_up, w_kv_up, q_a_norm, kv_a_norm, mla_qn, mla_kn,
                      ax_qn, ax_kn, win_qn, win_kn, w_br_mla, w_br_ax, w_br_win, w_out)
    mod_all = _modulation(jnp.concatenate([c_prompt, c_sample], axis=0), w_ada, b_ada)
    tabs = {}
    for S in {x_prompt.shape[1], x_sample.shape[1]}:
        pos = jnp.arange(S, dtype=F32)
        row = jnp.repeat(jnp.arange(S // GRID_W, dtype=F32), GRID_W)
        colp = jnp.tile(jnp.arange(GRID_W, dtype=F32), S // GRID_W)
        tabs[S] = _rope_tables(pos, pos) + _rope_tables(row, colp)
    bias_tab = _window_bias_table(rel_bias)
    G = WIN_HEADS // WIN_KV
    sink_tab = jnp.repeat(win_sink.astype(F32) * LOG2E, BLOCK, axis=-1).reshape(L, WIN_KV, G * BLOCK, 1)
    yp, ys = x_prompt, x_sample
    for l in range(L):
        mod_p = mod_all[l, :Bp][:, None, :]
        mod_s = mod_all[l, Bp:][:, None, :]
        yp = _layer(yp, mod_p, l, ln_g, w, tabs, bias_tab, sink_tab)
        ys = _layer(ys, mod_s, l, ln_g, w, tabs, bias_tab, sink_tab)
    return (yp, ys)
```

```python
import functools
import math

import jax
import jax.numpy as jnp
from jax import lax
from jax.experimental import pallas as pl
from jax.experimental.pallas import tpu as pltpu

F32 = jnp.float32
BF16 = jnp.bfloat16

GRID_W = 64
BLOCK = 128
EPS = 1e-6
ROPE_THETA = 10000.0
NEG = -1e30
LOG2E = 1.4426950408889634
LANES = 128

MLA_HEADS = 8
MLA_NOPE = 128
MLA_ROPE = 64
MLA_QK = MLA_NOPE + MLA_ROPE
MLA_V = 128
Q_LORA = 512
KV_LORA = 256
AX_HEADS = 8
AX_KV = 2
AX_HD = 128
WIN_HEADS = 8
WIN_KV = 2
WIN_HD = 128
WINDOW = 128
N_BUCKETS = 32
MAX_DIST = 128
ROPE_HALF = 32

VMEM_LIMIT = 56 * 1024 * 1024


def _cparams(sem):
    return pltpu.CompilerParams(dimension_semantics=sem, vmem_limit_bytes=VMEM_LIMIT)


def _pick(n, pref):
    t = min(pref, n)
    while n % t:
        t //= 2
    return t


def _mod_kernel(c_ref, w_ref, b_ref, o_ref):
    o_ref[0] = jnp.dot(c_ref[...], w_ref[0].astype(BF16), preferred_element_type=F32) + b_ref[0]


def _modulation(c_all, w_ada, b_ada):
    L, D, N = w_ada.shape
    R = c_all.shape[0]
    tn = _pick(N, 512)
    return pl.pallas_call(
        _mod_kernel,
        grid=(L, N // tn),
        in_specs=[pl.BlockSpec((R, D), lambda l, j: (0, 0)),
                  pl.BlockSpec((1, D, tn), lambda l, j: (l, 0, j)),
                  pl.BlockSpec((1, 1, tn), lambda l, j: (l, 0, j))],
        out_specs=pl.BlockSpec((1, R, tn), lambda l, j: (l, 0, j)),
        out_shape=jax.ShapeDtypeStruct((L, R, N), F32),
        compiler_params=_cparams(("parallel", "parallel")),
        name="adaln_mod",
    )(c_all.astype(BF16), w_ada, b_ada.reshape(L, 1, N))


def _h_kernel(x_ref, g_ref, sh_ref, sc_ref, o_ref):
    x = x_ref[0]
    r = lax.rsqrt(jnp.mean(x * x, axis=-1, keepdims=True) + EPS)
    y = x * r * g_ref[...]
    o_ref[0] = (y * (1.0 + sc_ref[0]) + sh_ref[0]).astype(BF16)


def _norm_modulate(x, ln_g, mod):
    B, S, D = x.shape
    ts = _pick(S, 512)
    return pl.pallas_call(
        _h_kernel,
        grid=(B, S // ts),
        in_specs=[pl.BlockSpec((1, ts, D), lambda b, i: (b, i, 0)),
                  pl.BlockSpec((1, D), lambda b, i: (0, 0)),
                  pl.BlockSpec((1, 1, D), lambda b, i: (b, 0, 0)),
                  pl.BlockSpec((1, 1, D), lambda b, i: (b, 0, 1))],
        out_specs=pl.BlockSpec((1, ts, D), lambda b, i: (b, i, 0)),
        out_shape=jax.ShapeDtypeStruct((B, S, D), BF16),
        compiler_params=_cparams(("parallel", "parallel")),
        name="norm_modulate",
    )(x, ln_g.reshape(1, D), mod, mod)


def _swap32(y):
    lane = lax.broadcasted_iota(jnp.int32, y.shape, 1)
    return jnp.where((lane % (2 * ROPE_HALF)) < ROPE_HALF,
                     pltpu.roll(y, LANES - ROPE_HALF, 1), pltpu.roll(y, ROPE_HALF, 1))


def _rope128(y, cos, sin_signed):
    return y * cos + _swap32(y) * sin_signed


def _rope_tables(pos_a, pos_b):
    freqs = ROPE_THETA ** (-jnp.arange(ROPE_HALF, dtype=F32) / ROPE_HALF)

    def seg(pos):
        ang = pos[:, None] * freqs[None, :]
        c, s = jnp.cos(ang), jnp.sin(ang)
        return jnp.concatenate([c, c], axis=-1), jnp.concatenate([-s, s], axis=-1)

    ca, sa = seg(pos_a)
    cb, sb = seg(pos_b)
    return jnp.concatenate([ca, cb], axis=-1), jnp.concatenate([sa, sb], axis=-1)


def _mla_prep_kernel(h_ref, wlat_ref, wq_ref, wkv_ref, gqa_ref, gkva_ref, gq_ref, gk_ref,
                     cos_ref, sin_ref, q_ref, k_ref, v_ref, *, rc):
    for r in range(h_ref.shape[0] // rc):
        _mla_prep_rows(slice(rc * r, rc * (r + 1)), h_ref, wlat_ref, wq_ref, wkv_ref, gqa_ref, gkva_ref,
                       gq_ref, gk_ref, cos_ref, sin_ref, q_ref, k_ref, v_ref)


def _mla_prep_rows(rows, h_ref, wlat_ref, wq_ref, wkv_ref, gqa_ref, gkva_ref, gq_ref, gk_ref,
                   cos_ref, sin_ref, q_ref, k_ref, v_ref):
    lat = jnp.dot(h_ref[rows, :], wlat_ref[...], preferred_element_type=F32)
    cq = lat[:, :Q_LORA]
    ckv = lat[:, Q_LORA:Q_LORA + KV_LORA]
    kr2 = lat[:, Q_LORA + KV_LORA:]
    cqn = cq * lax.rsqrt(jnp.mean(cq * cq, axis=-1, keepdims=True) + EPS) * gqa_ref[...]
    ckvn = ckv * lax.rsqrt(jnp.mean(ckv * ckv, axis=-1, keepdims=True) + EPS) * gkva_ref[...]
    qraw = jnp.dot(cqn.astype(BF16), wq_ref[...], preferred_element_type=F32)
    kvraw = jnp.dot(ckvn.astype(BF16), wkv_ref[...], preferred_element_type=F32)
    cos = cos_ref[rows, :]
    sin = sin_ref[rows, :]
    lane = lax.broadcasted_iota(jnp.int32, cos.shape, 1)
    lo = lane < MLA_ROPE
    gq = gq_ref[...]
    gk = gk_ref[...]
    nope_w = MLA_HEADS * MLA_NOPE
    inv_qk = 1.0 / MLA_QK
    kr_sq = kr2 * kr2
    ss_kr = jnp.sum(jnp.where(lo, kr_sq, 0.0), axis=-1, keepdims=True)
    k_rope = _rope128(kr2 * gk[:, LANES:], cos, sin)
    for p in range(MLA_HEADS // 2):
        qr = qraw[:, nope_w + LANES * p: nope_w + LANES * (p + 1)]
        qr_sq = qr * qr
        ss_r = (jnp.sum(jnp.where(lo, qr_sq, 0.0), axis=-1, keepdims=True),
                jnp.sum(jnp.where(lo, 0.0, qr_sq), axis=-1, keepdims=True))
        q_rope = _rope128(qr * gq[:, LANES:], cos, sin)
        for e in range(2):
            hd = 2 * p + e
            keep = lo if e == 0 else jnp.logical_not(lo)
            qn = qraw[:, LANES * hd: LANES * (hd + 1)]
            rq = lax.rsqrt((jnp.sum(qn * qn, axis=-1, keepdims=True) + ss_r[e]) * inv_qk + EPS)
            q_ref[rows, 2 * LANES * hd: 2 * LANES * hd + LANES] = (qn * rq * gq[:, :LANES]).astype(BF16)
            q_ref[rows, 2 * LANES * hd + LANES: 2 * LANES * (hd + 1)] = jnp.where(keep, q_rope * rq, 0.0).astype(BF16)
            kn = kvraw[:, LANES * hd: LANES * (hd + 1)]
            rk = lax.rsqrt((jnp.sum(kn * kn, axis=-1, keepdims=True) + ss_kr) * inv_qk + EPS)
            k_ref[rows, 2 * LANES * hd: 2 * LANES * hd + LANES] = (kn * rk * gk[:, :LANES]).astype(BF16)
            k_ref[rows, 2 * LANES * hd + LANES: 2 * LANES * (hd + 1)] = jnp.where(keep, k_rope * rk, 0.0).astype(BF16)
    v_ref[rows, :] = kvraw[:, nope_w:].astype(BF16)


def _mla_prep(h2, wlat, wq, wkv, gqa, gkva, gq, gk, cos, sin, S):
    T, D = h2.shape
    tm = _pick(S, 1024)
    nS = S // tm
    nlat = wlat.shape[1]
    hq = MLA_HEADS * 2 * LANES
    const = lambda i: (0, 0)
    return pl.pallas_call(
        functools.partial(_mla_prep_kernel, rc=min(128, tm)),
        grid=(T // tm,),
        in_specs=[pl.BlockSpec((tm, D), lambda i: (i, 0)),
                  pl.BlockSpec((D, nlat), const),
                  pl.BlockSpec(wq.shape, const),
                  pl.BlockSpec(wkv.shape, const),
                  pl.BlockSpec((1, Q_LORA), const),
                  pl.BlockSpec((1, KV_LORA), const),
                  pl.BlockSpec((1, 2 * LANES), const),
                  pl.BlockSpec((1, 2 * LANES), const),
                  pl.BlockSpec((tm, LANES), lambda i: (i % nS, 0)),
                  pl.BlockSpec((tm, LANES), lambda i: (i % nS, 0))],
        out_specs=[pl.BlockSpec((tm, hq), lambda i: (i, 0)),
                   pl.BlockSpec((tm, hq), lambda i: (i, 0)),
                   pl.BlockSpec((tm, MLA_HEADS * MLA_V), lambda i: (i, 0))],
        out_shape=[jax.ShapeDtypeStruct((T, hq), BF16),
                   jax.ShapeDtypeStruct((T, hq), BF16),
                   jax.ShapeDtypeStruct((T, MLA_HEADS * MLA_V), BF16)],
        compiler_params=_cparams(("parallel",)),
        name="mla_prep",
    )(h2, wlat, wq, wkv, gqa, gkva, gq, gk, cos, sin)


QKV_KINDS = (0, 0, 0, 0, 0, 2, 1, 1, 1, 1, 1, 2)


def _qkv_kernel(h_ref, w_ref, gain_ref, cos_ref, sin_ref, o_ref, *, rc):
    tm = h_ref.shape[0]
    tn = 2 * LANES
    for r in range(tm // rc):
        rows = slice(rc * r, rc * (r + 1))
        for t, kind in enumerate(QKV_KINDS):
            acc = jnp.dot(h_ref[rows, :], w_ref[:, tn * t: tn * (t + 1)], preferred_element_type=F32)
            if kind == 2:
                o_ref[rows, tn * t: tn * (t + 1)] = acc.astype(BF16)
                continue
            for e in range(2):
                lanes = slice(tn * t + LANES * e, tn * t + LANES * (e + 1))
                x = acc[:, LANES * e: LANES * (e + 1)]
                y = x * lax.rsqrt(jnp.mean(x * x, axis=-1, keepdims=True) + EPS) * gain_ref[:, lanes]
                if kind == 0:
                    y = _rope128(y, cos_ref[rows, :], sin_ref[rows, :])
                o_ref[rows, lanes] = y.astype(BF16)


def _qkv_proj(h2, w, gains, cos, sin, S):
    T, D = h2.shape
    N = w.shape[1]
    tm = _pick(S, 512)
    nS = S // tm
    return pl.pallas_call(
        functools.partial(_qkv_kernel, rc=min(256, tm)),
        grid=(T // tm,),
        in_specs=[pl.BlockSpec((tm, D), lambda i: (i, 0)),
                  pl.BlockSpec((D, N), lambda i: (0, 0), pipeline_mode=pl.Buffered(1)),
                  pl.BlockSpec((1, N), lambda i: (0, 0)),
                  pl.BlockSpec((tm, LANES), lambda i: (i % nS, 0)),
                  pl.BlockSpec((tm, LANES), lambda i: (i % nS, 0))],
        out_specs=pl.BlockSpec((tm, N), lambda i: (i, 0)),
        out_shape=jax.ShapeDtypeStruct((T, N), BF16),
        compiler_params=_cparams(("parallel",)),
        name="qkv_proj",
    )(h2, w, gains, cos, sin)


def _gate_kernel(h_ref, w_ref, o_ref):
    o_ref[...] = jnp.dot(h_ref[...], w_ref[...], preferred_element_type=F32)


def _gate_proj(h2, w):
    T, D = h2.shape
    N = w.shape[1]
    tm = _pick(T, 2048)
    tn = _pick(N, 512)
    return pl.pallas_call(
        _gate_kernel,
        grid=(T // tm, N // tn),
        in_specs=[pl.BlockSpec((tm, D), lambda i, j: (i, 0)),
                  pl.BlockSpec((D, tn), lambda i, j: (0, j))],
        out_specs=pl.BlockSpec((tm, tn), lambda i, j: (i, j)),
        out_shape=jax.ShapeDtypeStruct((T, N), F32),
        compiler_params=_cparams(("parallel", "arbitrary")),
        name="gate_proj",
    )(h2, w)


VT_ROWS = LANES + 16


def _flash_kernel(q_ref, k_ref, v_ref, g_ref, o_ref, vt_ref, *, G, dk, tq, tk, S):
    qi = pl.program_id(2)
    n = S // tk

    @pl.when(qi == 0)
    def _():
        for c in range(n):
            keys = slice(tk * c, tk * (c + 1))
            vt_ref[:LANES, keys] = v_ref[0, keys, :].astype(F32).T.astype(BF16)
        row = lax.broadcasted_iota(jnp.int32, (VT_ROWS - LANES, S), 0)
        vt_ref[LANES:, :] = jnp.where(row == 0, 1.0, 0.0).astype(BF16)

    if G == 1:
        q = q_ref[0]
    else:
        q = jnp.concatenate([q_ref[0, :, dk * g: dk * (g + 1)] for g in range(G)], axis=0)

    def scores(c):
        k = k_ref[0, tk * c: tk * (c + 1), :]
        return lax.dot_general(k, q, (((1,), (1,)), ((), ())), preferred_element_type=F32)

    s_next = scores(0)
    m = None
    acc = None
    for c in range(n):
        s = s_next
        if c + 1 < n:
            s_next = scores(c + 1)
        m_blk = jnp.max(s, axis=0, keepdims=True)
        m_new = m_blk if m is None else jnp.maximum(m, m_blk)
        p = jnp.exp2(s - m_new).astype(BF16)
        pv = jnp.dot(vt_ref[:, tk * c: tk * (c + 1)], p, preferred_element_type=F32)
        acc = pv if acc is None else jnp.exp2(m - m_new) * acc + pv
        m = m_new
    o = (acc[:LANES, :] / acc[LANES:LANES + 1, :]).T
    for g in range(G):
        gate = g_ref[0, :, LANES * g: LANES * (g + 1)]
        o_ref[0, :, LANES * g: LANES * (g + 1)] = (o[tq * g: tq * (g + 1)] * (gate * jax.nn.sigmoid(gate))).astype(BF16)


def _flash_attn(q_arr, q_blk0, k_arr, k_blk0, v_arr, v_blk0, g_arr, g_blk0, *, B, S, H, KV, dk, rows):
    G = H // KV
    tq = rows // G
    tk = _pick(S, 1024)
    kern = functools.partial(_flash_kernel, G=G, dk=dk, tq=tq, tk=tk, S=S)
    return pl.pallas_call(
        kern,
        grid=(B, KV, S // tq),
        in_specs=[pl.BlockSpec((1, tq, G * dk), lambda b, h, i: (b, i, q_blk0 + h)),
                  pl.BlockSpec((1, S, dk), lambda b, h, i: (b, 0, k_blk0 + h)),
                  pl.BlockSpec((1, S, LANES), lambda b, h, i: (b, 0, v_blk0 + h)),
                  pl.BlockSpec((1, tq, G * LANES), lambda b, h, i: (b, i, g_blk0 + h))],
        out_specs=pl.BlockSpec((1, tq, G * LANES), lambda b, h, i: (b, i, h)),
        out_shape=jax.ShapeDtypeStruct((B, S, H * LANES), BF16),
        scratch_shapes=[pltpu.VMEM((VT_ROWS, S), BF16)],
        compiler_params=_cparams(("parallel", "parallel", "arbitrary")),
        name=f"flash_attn_g{G}",
    )(q_arr, k_arr, v_arr, g_arr)


def _win_kernel(q_ref, kp_ref, kc_ref, kn_ref, vp_ref, vc_ref, vn_ref, bias_ref, sink_ref, g_ref, o_ref, *, G, nq, nsteps):
    j = pl.program_id(2)
    nt = (((1,), (1,)), ((), ()))
    sink = sink_ref[0]

    def kv_block(prev_ref, cur_ref, next_ref, t):
        if t < 0:
            return prev_ref[0]
        if t >= nq:
            return next_ref[0]
        return cur_ref[0, BLOCK * t: BLOCK * (t + 1), :]

    for t in range(nq):
        rows = slice(BLOCK * t, BLOCK * (t + 1))
        q = jnp.concatenate([q_ref[0, rows, LANES * g: LANES * (g + 1)] for g in range(G)], axis=0)
        s = [lax.dot_general(q, kv_block(kp_ref, kc_ref, kn_ref, t - 1 + d), nt, preferred_element_type=F32)
             + bias_ref[0, :, BLOCK * d: BLOCK * (d + 1)] for d in range(3)]
        if t == 0:
            s[0] = s[0] + jnp.where(j == 0, NEG, 0.0)
        if t == nq - 1:
            s[2] = s[2] + jnp.where(j == nsteps - 1, NEG, 0.0)
        m = jnp.maximum(jnp.max(jnp.maximum(jnp.maximum(s[0], s[1]), s[2]), axis=1, keepdims=True), sink)
        p = [jnp.exp2(sd - m) for sd in s]
        l = jnp.sum(p[0] + p[1] + p[2], axis=1, keepdims=True) + jnp.exp2(sink - m)
        o = sum(jnp.dot(p[d].astype(BF16), kv_block(vp_ref, vc_ref, vn_ref, t - 1 + d), preferred_element_type=F32)
                for d in range(3)) / l
        for g in range(G):
            gate = g_ref[0, rows, LANES * g: LANES * (g + 1)]
            o_ref[0, rows, LANES * g: LANES * (g + 1)] = (
                o[BLOCK * g: BLOCK * (g + 1)] * (gate * jax.nn.sigmoid(gate))).astype(BF16)


def _window_attn(qkv, q_blk0, k_blk0, v_blk0, gate, g_blk0, bias, sink, *, B, S):
    G = WIN_HEADS // WIN_KV
    nb = S // BLOCK
    nq = 8 if nb % 8 == 0 else (4 if nb % 4 == 0 else 1)
    nsteps = nb // nq
    kern = functools.partial(_win_kernel, G=G, nq=nq, nsteps=nsteps)

    def kv_specs(blk0):
        return [pl.BlockSpec((1, BLOCK, LANES), lambda b, h, j: (b, jnp.maximum(j * nq - 1, 0), blk0 + h)),
                pl.BlockSpec((1, nq * BLOCK, LANES), lambda b, h, j: (b, j, blk0 + h)),
                pl.BlockSpec((1, BLOCK, LANES), lambda b, h, j: (b, jnp.minimum((j + 1) * nq, nb - 1), blk0 + h))]

    return pl.pallas_call(
        kern,
        grid=(B, WIN_KV, nsteps),
        in_specs=[pl.BlockSpec((1, nq * BLOCK, G * LANES), lambda b, h, j: (b, j, q_blk0 + h))]
        + kv_specs(k_blk0) + kv_specs(v_blk0)
        + [pl.BlockSpec((1, G * BLOCK, 3 * BLOCK), lambda b, h, j: (h, 0, 0)),
           pl.BlockSpec((1, G * BLOCK, 1), lambda b, h, j: (h, 0, 0)),
           pl.BlockSpec((1, nq * BLOCK, G * LANES), lambda b, h, j: (b, j, g_blk0 + h))],
        out_specs=pl.BlockSpec((1, nq * BLOCK, G * LANES), lambda b, h, j: (b, j, h)),
        out_shape=jax.ShapeDtypeStruct((B, S, WIN_HEADS * LANES), BF16),
        compiler_params=_cparams(("parallel", "parallel", "arbitrary")),
        name="window_attn",
    )(qkv, qkv, qkv, qkv, qkv, qkv, qkv, bias, sink, gate)


def _t5_bucket(rel):
    half = N_BUCKETS // 2
    max_exact = half // 2
    ret = jnp.where(rel > 0, half, 0)
    n = jnp.abs(rel)
    nf = jnp.maximum(n, 1).astype(F32)
    large = max_exact + (jnp.log(nf / max_exact) / math.log(MAX_DIST / max_exact)
                         * (half - max_exact)).astype(jnp.int32)
    large = jnp.minimum(large, half - 1)
    return ret + jnp.where(n < max_exact, n, large)


def _bias_kernel(bucket_ref, rb_ref, o_ref):
    bucket = bucket_ref[...]
    row = lax.broadcasted_iota(jnp.int32, bucket.shape, 0)
    colk = lax.broadcasted_iota(jnp.int32, bucket.shape, 1)
    inband = jnp.abs(colk - WINDOW - row) <= WINDOW
    for h in range(WIN_HEADS):
        acc = jnp.zeros(bucket.shape, F32)
        for b in range(N_BUCKETS):
            acc = jnp.where(bucket == b, rb_ref[b, h], acc)
        o_ref[h] = jnp.where(inband, acc * LOG2E, NEG)


def _window_bias_table(rel_bias):
    span = BLOCK + 2 * WINDOW
    rel = jnp.arange(span)[None, :] - WINDOW - jnp.arange(BLOCK)[:, None]
    bias = pl.pallas_call(
        _bias_kernel,
        in_specs=[pl.BlockSpec(memory_space=pltpu.VMEM), pl.BlockSpec(memory_space=pltpu.SMEM)],
        out_specs=pl.BlockSpec(memory_space=pltpu.VMEM),
        out_shape=jax.ShapeDtypeStruct((WIN_HEADS, BLOCK, span), F32),
        name="t5_bias_table",
    )(_t5_bucket(rel).astype(jnp.int32), rel_bias.astype(F32))
    G = WIN_HEADS // WIN_KV
    return bias.reshape(WIN_KV, G * BLOCK, span)


def _branch_kernel(a1, a2, a3, w1, w2, w3, m1, m2, m3, o_ref):
    y = jax.nn.sigmoid(m1[...]) * jnp.dot(a1[...], w1[...], preferred_element_type=F32)
    y = y + jax.nn.sigmoid(m2[...]) * jnp.dot(a2[...], w2[...], preferred_element_type=F32)
    y = y + jax.nn.sigmoid(m3[...]) * jnp.dot(a3[...], w3[...], preferred_element_type=F32)
    o_ref[...] = y.astype(BF16)


def _branch_merge(a_list, w_list, gate, m_col0):
    T, W = a_list[0].shape
    D = w_list[0].shape[1]
    tm = _pick(T, 512)
    tn = _pick(D, 1024)
    nj = D // tn
    a_spec = pl.BlockSpec((tm, W), lambda j, i: (i, 0))
    w_spec = pl.BlockSpec((W, tn), lambda j, i: (0, j))

    def m_spec(k):
        base = (m_col0 + k * D) // tn
        return pl.BlockSpec((tm, tn), lambda j, i: (i, base + j))

    return pl.pallas_call(
        _branch_kernel,
        grid=(nj, T // tm),
        in_specs=[a_spec] * 3 + [w_spec] * 3 + [m_spec(0), m_spec(1), m_spec(2)],
        out_specs=pl.BlockSpec((tm, tn), lambda j, i: (i, j)),
        out_shape=jax.ShapeDtypeStruct((T, D), BF16),
        compiler_params=_cparams(("parallel", "arbitrary")),
        name="branch_merge",
    )(*a_list, *w_list, gate, gate, gate)


def _out_kernel(y_ref, w_ref, x_ref, gt_ref, *rest, rc, with_next):
    if with_next:
        g_ref, sh_ref, sc_ref, o_ref, hn_ref = rest
    else:
        (o_ref,) = rest
    for r in range(y_ref.shape[1] // rc):
        rows = slice(rc * r, rc * (r + 1))
        x = x_ref[0, rows, :] + gt_ref[0] * jnp.dot(y_ref[0, rows, :], w_ref[...], preferred_element_type=F32)
        o_ref[0, rows, :] = x
        if with_next:
            rms = lax.rsqrt(jnp.mean(x * x, axis=-1, keepdims=True) + EPS)
            hn_ref[0, rows, :] = ((x * rms * g_ref[...]) * (1.0 + sc_ref[0]) + sh_ref[0]).astype(BF16)


def _out_proj(y, w_out, x, mod, ln_g_next=None, mod_next=None):
    B, S, D = x.shape
    tm = _pick(S, 512)
    with_next = ln_g_next is not None
    row_spec = pl.BlockSpec((1, tm, D), lambda b, i: (b, i, 0))
    in_specs = [row_spec,
                pl.BlockSpec((D, D), lambda b, i: (0, 0)),
                row_spec,
                pl.BlockSpec((1, 1, D), lambda b, i: (b, 0, 2))]
    args = [y, w_out, x, mod]
    out_specs = [row_spec]
    out_shape = [jax.ShapeDtypeStruct((B, S, D), F32)]
    if with_next:
        in_specs += [pl.BlockSpec((1, D), lambda b, i: (0, 0)),
                     pl.BlockSpec((1, 1, D), lambda b, i: (b, 0, 0)),
                     pl.BlockSpec((1, 1, D), lambda b, i: (b, 0, 1))]
        args += [ln_g_next.reshape(1, D), mod_next, mod_next]
        out_specs.append(row_spec)
        out_shape.append(jax.ShapeDtypeStruct((B, S, D), BF16))
    res = pl.pallas_call(
        functools.partial(_out_kernel, rc=min(128, tm), with_next=with_next),
        grid=(B, S // tm),
        in_specs=in_specs,
        out_specs=out_specs,
        out_shape=out_shape,
        compiler_params=_cparams(("parallel", "parallel")),
        name="out_proj",
    )(*args)
    return (res[0], res[1]) if with_next else (res[0], None)


def _split_points(D):
    sizes = [Q_LORA, KV_LORA, MLA_ROPE, MLA_HEADS * MLA_V,
             AX_HEADS * AX_HD, AX_KV * AX_HD, AX_KV * AX_HD, AX_HEADS * AX_HD,
             WIN_HEADS * WIN_HD, WIN_KV * WIN_HD, WIN_KV * WIN_HD, WIN_HEADS * WIN_HD,
             D, D, D]
    pts, acc = [0], 0
    for s in sizes:
        acc += s
        pts.append(acc)
    return pts


def _prep_weights(D, w_in, w_q_up, w_kv_up, q_a_norm, kv_a_norm, mla_qn, mla_kn,
                  ax_qn, ax_kn, win_qn, win_kn, w_br_mla, w_br_ax, w_br_win, w_out):
    p = _split_points(D)
    col = lambda k: w_in[:, :, p[k]:p[k + 1]]
    (cq, ckv, kr, g_mla, aq, ak, av, g_ax, wq, wk, wv, g_win, m_mla, m_ax, m_win) = [col(k) for k in range(15)]
    L = w_in.shape[0]
    w = {}
    w["lat"] = jnp.concatenate([cq, ckv, kr, kr], axis=-1).astype(BF16)
    w["qkv"] = jnp.concatenate([aq, ak, av, wq, wk, wv], axis=-1).astype(BF16)
    w["gate"] = jnp.concatenate([g_mla, g_ax, g_win, m_mla, m_ax, m_win], axis=-1).astype(BF16)
    wq4 = w_q_up.reshape(L, Q_LORA, MLA_HEADS, MLA_QK)
    w["q_up"] = jnp.concatenate([wq4[..., :MLA_NOPE].reshape(L, Q_LORA, -1),
                                 wq4[..., MLA_NOPE:].reshape(L, Q_LORA, -1)], axis=-1).astype(BF16)
    wkv4 = w_kv_up.reshape(L, KV_LORA, MLA_HEADS, MLA_NOPE + MLA_V)
    w["kv_up"] = jnp.concatenate([wkv4[..., :MLA_NOPE].reshape(L, KV_LORA, -1),
                                  wkv4[..., MLA_NOPE:].reshape(L, KV_LORA, -1)], axis=-1).astype(BF16)
    w["gqa"] = q_a_norm.reshape(L, 1, Q_LORA)
    w["gkva"] = kv_a_norm.reshape(L, 1, KV_LORA)
    qs_mla = MLA_QK ** -0.5 * LOG2E
    w["gq"] = (jnp.concatenate([mla_qn, mla_qn[:, MLA_NOPE:]], axis=-1) * qs_mla).reshape(L, 1, 2 * LANES)
    w["gk"] = jnp.concatenate([mla_kn, mla_kn[:, MLA_NOPE:]], axis=-1).reshape(L, 1, 2 * LANES)
    ones = jnp.ones((L, AX_KV * AX_HD), F32)
    w["qkv_gain"] = jnp.concatenate(
        [jnp.tile(ax_qn, (1, AX_HEADS)) * (AX_HD ** -0.5 * LOG2E), jnp.tile(ax_kn, (1, AX_KV)), ones,
         jnp.tile(win_qn, (1, WIN_HEADS)) * (WIN_HD ** -0.5 * LOG2E), jnp.tile(win_kn, (1, WIN_KV)), ones],
        axis=-1).reshape(L, 1, -1)
    w["br"] = [w_br_mla.astype(BF16), w_br_ax.astype(BF16), w_br_win.astype(BF16)]
    w["out"] = w_out.astype(BF16)
    return w


def _layer(x, h, mod, mod_next, l, ln_g, w, tabs, bias_tab, sink_tab):
    B, S, D = x.shape
    T = B * S
    h2 = h.reshape(T, D)
    cos_m, sin_m, cos_a, sin_a = tabs[S]
    q_m, k_m, v_m = _mla_prep(h2, w["lat"][l], w["q_up"][l], w["kv_up"][l], w["gqa"][l], w["gkva"][l],
                              w["gq"][l], w["gk"][l], cos_m, sin_m, S)
    qkv = _qkv_proj(h2, w["qkv"][l], w["qkv_gain"][l], cos_a, sin_a, S)
    gate = _gate_proj(h2, w["gate"][l])
    r3 = lambda a: a.reshape(B, S, a.shape[-1])
    q_m, k_m, v_m, qkv3, gate3 = r3(q_m), r3(k_m), r3(v_m), r3(qkv), r3(gate)
    a_mla = _flash_attn(q_m, 0, k_m, 0, v_m, 0, gate3, 0,
                        B=B, S=S, H=MLA_HEADS, KV=MLA_HEADS, dk=2 * LANES, rows=min(2048, S))
    gq = AX_HEADS // AX_KV
    a_ax = _flash_attn(qkv3, 0, qkv3, AX_HEADS, qkv3, AX_HEADS + AX_KV, gate3, AX_KV,
                       B=B, S=S, H=AX_HEADS, KV=AX_KV, dk=AX_HD, rows=min(2048, S))
    wbase = AX_HEADS + 2 * AX_KV
    a_win = _window_attn(qkv3, wbase // gq, wbase + WIN_HEADS, wbase + WIN_HEADS + WIN_KV,
                         gate3, 2 * AX_KV, bias_tab, sink_tab[l], B=B, S=S)
    a_list = [a.reshape(T, a.shape[-1]) for a in (a_mla, a_ax, a_win)]
    y = _branch_merge(a_list, [wb[l] for wb in w["br"]], gate, 3 * MLA_HEADS * MLA_V)
    if mod_next is None:
        return _out_proj(y.reshape(B, S, D), w["out"][l], x, mod)
    return _out_proj(y.reshape(B, S, D), w["out"][l], x, mod, ln_g[l + 1], mod_next)


def kernel(x_prompt, x_sample, c_prompt, c_sample, ln_g, w_ada, b_ada, w_in, q_a_norm, w_q_up, kv_a_norm, w_kv_up, mla_qn, mla_kn, ax_qn, ax_kn, win_qn, win_kn, win_sink, rel_bias, w_br_mla, w_br_ax, w_br_win, w_out):
    L = ln_g.shape[0]
    D = x_prompt.shape[-1]
    Bp = x_prompt.shape[0]
    w = _prep_weights(D, w_in, w_q_up, w_kv_up, q_a_norm, kv_a_norm, mla_qn, mla_kn,
                      ax_qn, ax_kn, win_qn, win_kn, w_br_mla, w_br_ax, w_br_win, w_out)
    mod_all = _modulation(jnp.concatenate([c_prompt, c_sample], axis=0), w_ada, b_ada)
    tabs = {}
    for S in {x_prompt.shape[1], x_sample.shape[1]}:
        pos = jnp.arange(S, dtype=F32)
        row = jnp.repeat(jnp.arange(S // GRID_W, dtype=F32), GRID_W)
        colp = jnp.tile(jnp.arange(GRID_W, dtype=F32), S // GRID_W)
        tabs[S] = _rope_tables(pos, pos) + _rope_tables(row, colp)
    bias_tab = _window_bias_table(rel_bias)
    G = WIN_HEADS // WIN_KV
    sink_tab = jnp.repeat(win_sink.astype(F32) * LOG2E, BLOCK, axis=-1).reshape(L, WIN_KV, G * BLOCK, 1)
    mods_p = [mod_all[l, :Bp][:, None, :] for l in range(L)]
    mods_s = [mod_all[l, Bp:][:, None, :] for l in range(L)]
    yp, ys = x_prompt, x_sample
    hp = _norm_modulate(yp, ln_g[0], mods_p[0])
    hs = _norm_modulate(ys, ln_g[0], mods_s[0])
    for l in range(L):
        nxt = l + 1 < L
        yp, hp = _layer(yp, hp, mods_p[l], mods_p[l + 1] if nxt else None, l, ln_g, w, tabs, bias_tab, sink_tab)
        ys, hs = _layer(ys, hs, mods_s[l], mods_s[l + 1] if nxt else None, l, ln_g, w, tabs, bias_tab, sink_tab)
    return (yp, ys)
```

```python
import functools
import math

import jax
import jax.numpy as jnp
from jax import lax
from jax.experimental import pallas as pl
from jax.experimental.pallas import tpu as pltpu

F32 = jnp.float32
BF16 = jnp.bfloat16

GRID_W = 64
BLOCK = 128
EPS = 1e-6
ROPE_THETA = 10000.0
NEG = -1e30
LOG2E = 1.4426950408889634
LANES = 128

MLA_HEADS = 8
MLA_NOPE = 128
MLA_ROPE = 64
MLA_QK = MLA_NOPE + MLA_ROPE
MLA_V = 128
Q_LORA = 512
KV_LORA = 256
AX_HEADS = 8
AX_KV = 2
AX_HD = 128
WIN_HEADS = 8
WIN_KV = 2
WIN_HD = 128
WINDOW = 128
N_BUCKETS = 32
MAX_DIST = 128
ROPE_HALF = 32

VMEM_LIMIT = 56 * 1024 * 1024
MLA_PREP_RC = 128


def _cparams(sem):
    return pltpu.CompilerParams(dimension_semantics=sem, vmem_limit_bytes=VMEM_LIMIT)


def _pick(n, pref):
    t = min(pref, n)
    while n % t:
        t //= 2
    return t


def _mod_kernel(c_ref, w_ref, b_ref, o_ref):
    o_ref[0] = jnp.dot(c_ref[...], w_ref[0].astype(BF16), preferred_element_type=F32) + b_ref[0]


def _modulation(c_all, w_ada, b_ada):
    L, D, N = w_ada.shape
    R = c_all.shape[0]
    tn = _pick(N, 512)
    return pl.pallas_call(
        _mod_kernel,
        grid=(L, N // tn),
        in_specs=[pl.BlockSpec((R, D), lambda l, j: (0, 0)),
                  pl.BlockSpec((1, D, tn), lambda l, j: (l, 0, j)),
                  pl.BlockSpec((1, 1, tn), lambda l, j: (l, 0, j))],
        out_specs=pl.BlockSpec((1, R, tn), lambda l, j: (l, 0, j)),
        out_shape=jax.ShapeDtypeStruct((L, R, N), F32),
        compiler_params=_cparams(("parallel", "parallel")),
        name="adaln_mod",
    )(c_all.astype(BF16), w_ada, b_ada.reshape(L, 1, N))


def _h_kernel(x_ref, g_ref, sh_ref, sc_ref, o_ref):
    x = x_ref[0]
    r = lax.rsqrt(jnp.mean(x * x, axis=-1, keepdims=True) + EPS)
    y = x * r * g_ref[...]
    o_ref[0] = (y * (1.0 + sc_ref[0]) + sh_ref[0]).astype(BF16)


def _norm_modulate(x, ln_g, mod):
    B, S, D = x.shape
    ts = _pick(S, 512)
    return pl.pallas_call(
        _h_kernel,
        grid=(B, S // ts),
        in_specs=[pl.BlockSpec((1, ts, D), lambda b, i: (b, i, 0)),
                  pl.BlockSpec((1, D), lambda b, i: (0, 0)),
                  pl.BlockSpec((1, 1, D), lambda b, i: (b, 0, 0)),
                  pl.BlockSpec((1, 1, D), lambda b, i: (b, 0, 1))],
        out_specs=pl.BlockSpec((1, ts, D), lambda b, i: (b, i, 0)),
        out_shape=jax.ShapeDtypeStruct((B, S, D), BF16),
        compiler_params=_cparams(("parallel", "parallel")),
        name="norm_modulate",
    )(x, ln_g.reshape(1, D), mod, mod)


def _pair_layout(x):
    shp = x.shape
    x = x.reshape(shp[:-1] + (shp[-1] // LANES, 2, 2, ROPE_HALF))
    return jnp.swapaxes(x, -3, -2).reshape(shp)


def _rope128(y, cos, sin_signed):
    return y * cos + pltpu.roll(y, LANES // 2, 1) * sin_signed


def _rope_tables(pos_a, pos_b):
    freqs = ROPE_THETA ** (-jnp.arange(ROPE_HALF, dtype=F32) / ROPE_HALF)
    ang_a = pos_a[:, None] * freqs[None, :]
    ang_b = pos_b[:, None] * freqs[None, :]
    cos = jnp.concatenate([jnp.cos(ang_a), jnp.cos(ang_b)], axis=-1)
    sin = jnp.concatenate([jnp.sin(ang_a), jnp.sin(ang_b)], axis=-1)
    return jnp.concatenate([cos, cos], axis=-1), jnp.concatenate([-sin, sin], axis=-1)


def _mla_prep_kernel(cqn_ref, ckvn_ref, kr_ref, wq_ref, wkv_ref, gq_ref, gk_ref,
                     cos_ref, sin_ref, q_ref, k_ref, v_ref, *, rc):
    for r in range(cqn_ref.shape[0] // rc):
        _mla_prep_rows(slice(rc * r, rc * (r + 1)), cqn_ref, ckvn_ref, kr_ref, wq_ref, wkv_ref,
                       gq_ref, gk_ref, cos_ref, sin_ref, q_ref, k_ref, v_ref)


def _mla_prep_rows(rows, cqn_ref, ckvn_ref, kr_ref, wq_ref, wkv_ref, gq_ref, gk_ref,
                   cos_ref, sin_ref, q_ref, k_ref, v_ref):
    kr2 = kr_ref[rows, :]
    qraw = jnp.dot(cqn_ref[rows, :], wq_ref[...], preferred_element_type=F32)
    kvraw = jnp.dot(ckvn_ref[rows, :], wkv_ref[...], preferred_element_type=F32)
    cos = cos_ref[rows, :]
    sin = sin_ref[rows, :]
    lane = lax.broadcasted_iota(jnp.int32, cos.shape, 1)
    even = (lane % (2 * ROPE_HALF)) < ROPE_HALF
    gq = gq_ref[...]
    gk = gk_ref[...]
    nope_w = MLA_HEADS * MLA_NOPE
    inv_qk = 1.0 / MLA_QK
    kr_sq = kr2 * kr2
    k_rope = _rope128(kr2 * gk[:, LANES:], cos, sin)
    for p in range(MLA_HEADS // 2):
        qr = qraw[:, nope_w + LANES * p: nope_w + LANES * (p + 1)]
        qr_sq = qr * qr
        q_rope = _rope128(qr * gq[:, LANES:], cos, sin)
        for e in range(2):
            hd = 2 * p + e
            keep = even if e == 0 else jnp.logical_not(even)
            qn = qraw[:, LANES * hd: LANES * (hd + 1)]
            rq = lax.rsqrt(jnp.sum(qn * qn + jnp.where(keep, qr_sq, 0.0), axis=-1, keepdims=True) * inv_qk + EPS)
            q_ref[rows, 2 * LANES * hd: 2 * LANES * hd + LANES] = (qn * rq * gq[:, :LANES]).astype(BF16)
            q_ref[rows, 2 * LANES * hd + LANES: 2 * LANES * (hd + 1)] = jnp.where(keep, q_rope * rq, 0.0).astype(BF16)
            kn = kvraw[:, LANES * hd: LANES * (hd + 1)]
            rk = lax.rsqrt(jnp.sum(kn * kn + jnp.where(keep, kr_sq, 0.0), axis=-1, keepdims=True) * inv_qk + EPS)
            k_ref[rows, 2 * LANES * hd: 2 * LANES * hd + LANES] = (kn * rk * gk[:, :LANES]).astype(BF16)
            k_ref[rows, 2 * LANES * hd + LANES: 2 * LANES * (hd + 1)] = jnp.where(keep, k_rope * rk, 0.0).astype(BF16)
    v_ref[rows, :] = kvraw[:, nope_w:].astype(BF16)


def _mla_prep(lat, lat_col0, kr2, wq, wkv, gq, gk, cos, sin, S):
    T = lat.shape[0]
    tm = _pick(S, 1024)
    nS = S // tm
    hq = MLA_HEADS * 2 * LANES
    const = lambda i: (0, 0)
    return pl.pallas_call(
        functools.partial(_mla_prep_kernel, rc=min(MLA_PREP_RC, tm)),
        grid=(T // tm,),
        in_specs=[pl.BlockSpec((tm, Q_LORA), lambda i: (i, lat_col0 // Q_LORA)),
                  pl.BlockSpec((tm, KV_LORA), lambda i: (i, (lat_col0 + Q_LORA) // KV_LORA)),
                  pl.BlockSpec((tm, LANES), lambda i: (i, 0)),
                  pl.BlockSpec(wq.shape, const),
                  pl.BlockSpec(wkv.shape, const),
                  pl.BlockSpec((1, 2 * LANES), const),
                  pl.BlockSpec((1, 2 * LANES), const),
                  pl.BlockSpec((tm, LANES), lambda i: (i % nS, 0)),
                  pl.BlockSpec((tm, LANES), lambda i: (i % nS, 0))],
        out_specs=[pl.BlockSpec((tm, hq), lambda i: (i, 0)),
                   pl.BlockSpec((tm, hq), lambda i: (i, 0)),
                   pl.BlockSpec((tm, MLA_HEADS * MLA_V), lambda i: (i, 0))],
        out_shape=[jax.ShapeDtypeStruct((T, hq), BF16),
                   jax.ShapeDtypeStruct((T, hq), BF16),
                   jax.ShapeDtypeStruct((T, MLA_HEADS * MLA_V), BF16)],
        compiler_params=_cparams(("parallel",)),
        name="mla_prep",
    )(lat, lat, kr2, wq, wkv, gq, gk, cos, sin)


QKV_KINDS = (0, 0, 0, 0, 0, 2, 1, 1, 1, 1, 1, 2)


def _qkv_kernel(h_ref, w_ref, gain_ref, cos_ref, sin_ref, o_ref, kr_ref, *, rc):
    tm = h_ref.shape[0]
    tn = 2 * LANES
    c0 = tn * len(QKV_KINDS)
    for r in range(tm // rc):
        rows = slice(rc * r, rc * (r + 1))
        for lo, width in ((c0, Q_LORA), (c0 + Q_LORA, KV_LORA)):
            x = jnp.dot(h_ref[rows, :], w_ref[:, lo: lo + width], preferred_element_type=F32)
            y = x * lax.rsqrt(jnp.mean(x * x, axis=-1, keepdims=True) + EPS) * gain_ref[:, lo: lo + width]
            o_ref[rows, lo: lo + width] = y.astype(BF16)
        kr_ref[rows, :] = jnp.dot(h_ref[rows, :], w_ref[:, c0 + Q_LORA + KV_LORA:], preferred_element_type=F32)
        for t, kind in enumerate(QKV_KINDS):
            acc = jnp.dot(h_ref[rows, :], w_ref[:, tn * t: tn * (t + 1)], preferred_element_type=F32)
            if kind == 2:
                o_ref[rows, tn * t: tn * (t + 1)] = acc.astype(BF16)
                continue
            for e in range(2):
                lanes = slice(tn * t + LANES * e, tn * t + LANES * (e + 1))
                x = acc[:, LANES * e: LANES * (e + 1)]
                y = x * lax.rsqrt(jnp.mean(x * x, axis=-1, keepdims=True) + EPS) * gain_ref[:, lanes]
                if kind == 0:
                    y = _rope128(y, cos_ref[rows, :], sin_ref[rows, :])
                o_ref[rows, lanes] = y.astype(BF16)


def _qkv_proj(h2, w, gains, cos, sin, S):
    T, D = h2.shape
    N = w.shape[1]
    No = N - LANES
    tm = _pick(S, 512)
    nS = S // tm
    return pl.pallas_call(
        functools.partial(_qkv_kernel, rc=min(256, tm)),
        grid=(T // tm,),
        in_specs=[pl.BlockSpec((tm, D), lambda i: (i, 0)),
                  pl.BlockSpec((D, N), lambda i: (0, 0), pipeline_mode=pl.Buffered(1)),
                  pl.BlockSpec((1, No), lambda i: (0, 0)),
                  pl.BlockSpec((tm, LANES), lambda i: (i % nS, 0)),
                  pl.BlockSpec((tm, LANES), lambda i: (i % nS, 0))],
        out_specs=[pl.BlockSpec((tm, No), lambda i: (i, 0)),
                   pl.BlockSpec((tm, LANES), lambda i: (i, 0))],
        out_shape=[jax.ShapeDtypeStruct((T, No), BF16),
                   jax.ShapeDtypeStruct((T, LANES), F32)],
        compiler_params=_cparams(("parallel",)),
        name="qkv_proj",
    )(h2, w, gains, cos, sin)


def _gate_kernel(h_ref, w_ref, o_ref):
    o_ref[...] = jnp.dot(h_ref[...], w_ref[...], preferred_element_type=F32)


def _gate_proj(h2, w):
    T, D = h2.shape
    N = w.shape[1]
    tm = _pick(T, 2048)
    tn = _pick(N, 512)
    return pl.pallas_call(
        _gate_kernel,
        grid=(T // tm, N // tn),
        in_specs=[pl.BlockSpec((tm, D), lambda i, j: (i, 0)),
                  pl.BlockSpec((D, tn), lambda i, j: (0, j))],
        out_specs=pl.BlockSpec((tm, tn), lambda i, j: (i, j)),
        out_shape=jax.ShapeDtypeStruct((T, N), F32),
        compiler_params=_cparams(("parallel", "arbitrary")),
        name="gate_proj",
    )(h2, w)


def _flash_kernel(q_ref, k_ref, v_ref, g_ref, o_ref, vext_ref, *, G, dk, tq, tk, S, nt):
    qi = pl.program_id(2)

    def build_vext():
        lane = lax.broadcasted_iota(jnp.int32, (S, LANES), 1)
        vext_ref[:, :LANES] = v_ref[0]
        vext_ref[:, LANES:] = jnp.where(lane == 0, 1.0, 0.0).astype(BF16)

    if nt * tq == S:
        build_vext()
    else:
        pl.when(qi == 0)(build_vext)
    n = S // tk
    for t in range(nt):
        qrows = slice(tq * t, tq * (t + 1))
        if G == 1:
            q = q_ref[0, qrows, :]
        else:
            q = jnp.concatenate([q_ref[0, qrows, dk * g: dk * (g + 1)] for g in range(G)], axis=0)

        def scores(c, q=q):
            k = k_ref[0, tk * c: tk * (c + 1), :]
            return lax.dot_general(q, k, (((1,), (1,)), ((), ())), preferred_element_type=F32)

        s_next = scores(0)
        m = None
        acc = None
        for c in range(n):
            s = s_next
            if c + 1 < n:
                s_next = scores(c + 1)
            m_blk = jnp.max(s, axis=1, keepdims=True)
            m_new = m_blk if m is None else jnp.maximum(m, m_blk)
            p = jnp.exp2(s - m_new).astype(BF16)
            pv = jnp.dot(p, vext_ref[tk * c: tk * (c + 1), :], preferred_element_type=F32)
            acc = pv if acc is None else jnp.exp2(m - m_new) * acc + pv
            m = m_new
        o = acc[:, :LANES] / acc[:, LANES:LANES + 1]
        for g in range(G):
            gate = g_ref[0, qrows, LANES * g: LANES * (g + 1)]
            o_ref[0, qrows, LANES * g: LANES * (g + 1)] = (
                o[tq * g: tq * (g + 1)] * (gate * jax.nn.sigmoid(gate))).astype(BF16)


def _flash_attn(q_arr, q_blk0, k_arr, k_blk0, v_arr, v_blk0, g_arr, g_blk0, *, B, S, H, KV, dk, rows):
    G = H // KV
    tq = rows // G
    tk = _pick(S, 1024)
    nt = min(2, S // tq)
    kern = functools.partial(_flash_kernel, G=G, dk=dk, tq=tq, tk=tk, S=S, nt=nt)
    return pl.pallas_call(
        kern,
        grid=(B, KV, S // (tq * nt)),
        in_specs=[pl.BlockSpec((1, tq * nt, G * dk), lambda b, h, i: (b, i, q_blk0 + h)),
                  pl.BlockSpec((1, S, dk), lambda b, h, i: (b, 0, k_blk0 + h)),
                  pl.BlockSpec((1, S, LANES), lambda b, h, i: (b, 0, v_blk0 + h)),
                  pl.BlockSpec((1, tq * nt, G * LANES), lambda b, h, i: (b, i, g_blk0 + h))],
        out_specs=pl.BlockSpec((1, tq * nt, G * LANES), lambda b, h, i: (b, i, h)),
        out_shape=jax.ShapeDtypeStruct((B, S, H * LANES), BF16),
        scratch_shapes=[pltpu.VMEM((S, 2 * LANES), BF16)],
        compiler_params=_cparams(("parallel", "parallel", "arbitrary")),
        name=f"flash_attn_g{G}",
    )(q_arr, k_arr, v_arr, g_arr)


def _win_kernel(q_ref, kp_ref, kc_ref, kn_ref, vp_ref, vc_ref, vn_ref, bias_ref, sink_ref, g_ref, o_ref, *, G, nq, nsteps):
    j = pl.program_id(2)
    nt = (((1,), (1,)), ((), ()))
    sink = sink_ref[0]

    def kv_block(prev_ref, cur_ref, next_ref, t):
        if t < 0:
            return prev_ref[0]
        if t >= nq:
            return next_ref[0]
        return cur_ref[0, BLOCK * t: BLOCK * (t + 1), :]

    for t in range(nq):
        rows = slice(BLOCK * t, BLOCK * (t + 1))
        q = jnp.concatenate([q_ref[0, rows, LANES * g: LANES * (g + 1)] for g in range(G)], axis=0)
        s = [lax.dot_general(q, kv_block(kp_ref, kc_ref, kn_ref, t - 1 + d), nt, preferred_element_type=F32)
             + bias_ref[0, :, BLOCK * d: BLOCK * (d + 1)] for d in range(3)]
        if t == 0:
            s[0] = s[0] + jnp.where(j == 0, NEG, 0.0)
        if t == nq - 1:
            s[2] = s[2] + jnp.where(j == nsteps - 1, NEG, 0.0)
        m = jnp.maximum(jnp.max(jnp.maximum(jnp.maximum(s[0], s[1]), s[2]), axis=1, keepdims=True), sink)
        p = [jnp.exp2(sd - m) for sd in s]
        l = jnp.sum(p[0] + p[1] + p[2], axis=1, keepdims=True) + jnp.exp2(sink - m)
        o = sum(jnp.dot(p[d].astype(BF16), kv_block(vp_ref, vc_ref, vn_ref, t - 1 + d), preferred_element_type=F32)
                for d in range(3)) / l
        for g in range(G):
            gate = g_ref[0, rows, LANES * g: LANES * (g + 1)]
            o_ref[0, rows, LANES * g: LANES * (g + 1)] = (
                o[BLOCK * g: BLOCK * (g + 1)] * (gate * jax.nn.sigmoid(gate))).astype(BF16)


def _window_attn(qkv, q_blk0, k_blk0, v_blk0, gate, g_blk0, bias, sink, *, B, S):
    G = WIN_HEADS // WIN_KV
    nb = S // BLOCK
    nq = 8 if nb % 8 == 0 else (4 if nb % 4 == 0 else 1)
    nsteps = nb // nq
    kern = functools.partial(_win_kernel, G=G, nq=nq, nsteps=nsteps)

    def kv_specs(blk0):
        return [pl.BlockSpec((1, BLOCK, LANES), lambda b, h, j: (b, jnp.maximum(j * nq - 1, 0), blk0 + h)),
                pl.BlockSpec((1, nq * BLOCK, LANES), lambda b, h, j: (b, j, blk0 + h)),
                pl.BlockSpec((1, BLOCK, LANES), lambda b, h, j: (b, jnp.minimum((j + 1) * nq, nb - 1), blk0 + h))]

    return pl.pallas_call(
        kern,
        grid=(B, WIN_KV, nsteps),
        in_specs=[pl.BlockSpec((1, nq * BLOCK, G * LANES), lambda b, h, j: (b, j, q_blk0 + h))]
        + kv_specs(k_blk0) + kv_specs(v_blk0)
        + [pl.BlockSpec((1, G * BLOCK, 3 * BLOCK), lambda b, h, j: (h, 0, 0)),
           pl.BlockSpec((1, G * BLOCK, 1), lambda b, h, j: (h, 0, 0)),
           pl.BlockSpec((1, nq * BLOCK, G * LANES), lambda b, h, j: (b, j, g_blk0 + h))],
        out_specs=pl.BlockSpec((1, nq * BLOCK, G * LANES), lambda b, h, j: (b, j, h)),
        out_shape=jax.ShapeDtypeStruct((B, S, WIN_HEADS * LANES), BF16),
        compiler_params=_cparams(("parallel", "parallel", "arbitrary")),
        name="window_attn",
    )(qkv, qkv, qkv, qkv, qkv, qkv, qkv, bias, sink, gate)


def _t5_bucket(rel):
    half = N_BUCKETS // 2
    max_exact = half // 2
    ret = jnp.where(rel > 0, half, 0)
    n = jnp.abs(rel)
    nf = jnp.maximum(n, 1).astype(F32)
    large = max_exact + (jnp.log(nf / max_exact) / math.log(MAX_DIST / max_exact)
                         * (half - max_exact)).astype(jnp.int32)
    large = jnp.minimum(large, half - 1)
    return ret + jnp.where(n < max_exact, n, large)


def _bias_kernel(bucket_ref, rb_ref, o_ref):
    bucket = bucket_ref[...]
    row = lax.broadcasted_iota(jnp.int32, bucket.shape, 0)
    colk = lax.broadcasted_iota(jnp.int32, bucket.shape, 1)
    inband = jnp.abs(colk - WINDOW - row) <= WINDOW
    for h in range(WIN_HEADS):
        acc = jnp.zeros(bucket.shape, F32)
        for b in range(N_BUCKETS):
            acc = jnp.where(bucket == b, rb_ref[b, h], acc)
        o_ref[h] = jnp.where(inband, acc * LOG2E, NEG)


def _window_bias_table(rel_bias):
    span = BLOCK + 2 * WINDOW
    rel = jnp.arange(span)[None, :] - WINDOW - jnp.arange(BLOCK)[:, None]
    bias = pl.pallas_call(
        _bias_kernel,
        in_specs=[pl.BlockSpec(memory_space=pltpu.VMEM), pl.BlockSpec(memory_space=pltpu.SMEM)],
        out_specs=pl.BlockSpec(memory_space=pltpu.VMEM),
        out_shape=jax.ShapeDtypeStruct((WIN_HEADS, BLOCK, span), F32),
        name="t5_bias_table",
    )(_t5_bucket(rel).astype(jnp.int32), rel_bias.astype(F32))
    G = WIN_HEADS // WIN_KV
    return bias.reshape(WIN_KV, G * BLOCK, span)


def _branch_kernel(a1, a2, a3, w1, w2, w3, m1, m2, m3, o_ref):
    y = jax.nn.sigmoid(m1[...]) * jnp.dot(a1[...], w1[...], preferred_element_type=F32)
    y = y + jax.nn.sigmoid(m2[...]) * jnp.dot(a2[...], w2[...], preferred_element_type=F32)
    y = y + jax.nn.sigmoid(m3[...]) * jnp.dot(a3[...], w3[...], preferred_element_type=F32)
    o_ref[...] = y.astype(BF16)


def _branch_merge(a_list, w_list, gate, m_col0):
    T, W = a_list[0].shape
    D = w_list[0].shape[1]
    tm = _pick(T, 512)
    tn = _pick(D, 1024)
    nj = D // tn
    a_spec = pl.BlockSpec((tm, W), lambda j, i: (i, 0))
    w_spec = pl.BlockSpec((W, tn), lambda j, i: (0, j))

    def m_spec(k):
        base = (m_col0 + k * D) // tn
        return pl.BlockSpec((tm, tn), lambda j, i: (i, base + j))

    return pl.pallas_call(
        _branch_kernel,
        grid=(nj, T // tm),
        in_specs=[a_spec] * 3 + [w_spec] * 3 + [m_spec(0), m_spec(1), m_spec(2)],
        out_specs=pl.BlockSpec((tm, tn), lambda j, i: (i, j)),
        out_shape=jax.ShapeDtypeStruct((T, D), BF16),
        compiler_params=_cparams(("parallel", "arbitrary")),
        name="branch_merge",
    )(*a_list, *w_list, gate, gate, gate)


def _out_kernel(y_ref, w_ref, x_ref, gt_ref, *rest, rc, with_next):
    if with_next:
        g_ref, sh_ref, sc_ref, o_ref, hn_ref = rest
    else:
        (o_ref,) = rest
    for r in range(y_ref.shape[1] // rc):
        rows = slice(rc * r, rc * (r + 1))
        x = x_ref[0, rows, :] + gt_ref[0] * jnp.dot(y_ref[0, rows, :], w_ref[...], preferred_element_type=F32)
        o_ref[0, rows, :] = x
        if with_next:
            rms = lax.rsqrt(jnp.mean(x * x, axis=-1, keepdims=True) + EPS)
            hn_ref[0, rows, :] = ((x * rms * g_ref[...]) * (1.0 + sc_ref[0]) + sh_ref[0]).astype(BF16)


def _out_proj(y, w_out, x, mod, ln_g_next=None, mod_next=None):
    B, S, D = x.shape
    tm = _pick(S, 512)
    with_next = ln_g_next is not None
    row_spec = pl.BlockSpec((1, tm, D), lambda b, i: (b, i, 0))
    in_specs = [row_spec,
                pl.BlockSpec((D, D), lambda b, i: (0, 0)),
                row_spec,
                pl.BlockSpec((1, 1, D), lambda b, i: (b, 0, 2))]
    args = [y, w_out, x, mod]
    out_specs = [row_spec]
    out_shape = [jax.ShapeDtypeStruct((B, S, D), F32)]
    if with_next:
        in_specs += [pl.BlockSpec((1, D), lambda b, i: (0, 0)),
                     pl.BlockSpec((1, 1, D), lambda b, i: (b, 0, 0)),
                     pl.BlockSpec((1, 1, D), lambda b, i: (b, 0, 1))]
        args += [ln_g_next.reshape(1, D), mod_next, mod_next]
        out_specs.append(row_spec)
        out_shape.append(jax.ShapeDtypeStruct((B, S, D), BF16))
    res = pl.pallas_call(
        functools.partial(_out_kernel, rc=min(128, tm), with_next=with_next),
        grid=(B, S // tm),
        in_specs=in_specs,
        out_specs=out_specs,
        out_shape=out_shape,
        compiler_params=_cparams(("parallel", "parallel")),
        name="out_proj",
    )(*args)
    return (res[0], res[1]) if with_next else (res[0], None)


def _split_points(D):
    sizes = [Q_LORA, KV_LORA, MLA_ROPE, MLA_HEADS * MLA_V,
             AX_HEADS * AX_HD, AX_KV * AX_HD, AX_KV * AX_HD, AX_HEADS * AX_HD,
             WIN_HEADS * WIN_HD, WIN_KV * WIN_HD, WIN_KV * WIN_HD, WIN_HEADS * WIN_HD,
             D, D, D]
    pts, acc = [0], 0
    for s in sizes:
        acc += s
        pts.append(acc)
    return pts


def _prep_weights(D, w_in, w_q_up, w_kv_up, q_a_norm, kv_a_norm, mla_qn, mla_kn,
                  ax_qn, ax_kn, win_qn, win_kn, w_br_mla, w_br_ax, w_br_win, w_out):
    p = _split_points(D)
    col = lambda k: w_in[:, :, p[k]:p[k + 1]]
    (cq, ckv, kr, g_mla, aq, ak, av, g_ax, wq, wk, wv, g_win, m_mla, m_ax, m_win) = [col(k) for k in range(15)]
    L = w_in.shape[0]
    w = {}
    w["qkv"] = jnp.concatenate([_pair_layout(aq), _pair_layout(ak), av, wq, wk, wv, cq, ckv,
                                _pair_layout(jnp.concatenate([kr, kr], axis=-1))], axis=-1).astype(BF16)
    w["gate"] = jnp.concatenate([g_mla, g_ax, g_win, m_mla, m_ax, m_win], axis=-1).astype(BF16)
    wq4 = w_q_up.reshape(L, Q_LORA, MLA_HEADS, MLA_QK)
    w["q_up"] = jnp.concatenate([wq4[..., :MLA_NOPE].reshape(L, Q_LORA, -1),
                                 _pair_layout(wq4[..., MLA_NOPE:].reshape(L, Q_LORA, -1))], axis=-1).astype(BF16)
    wkv4 = w_kv_up.reshape(L, KV_LORA, MLA_HEADS, MLA_NOPE + MLA_V)
    w["kv_up"] = jnp.concatenate([wkv4[..., :MLA_NOPE].reshape(L, KV_LORA, -1),
                                  wkv4[..., MLA_NOPE:].reshape(L, KV_LORA, -1)], axis=-1).astype(BF16)
    qs_mla = MLA_QK ** -0.5 * LOG2E
    pair_gain = lambda g: jnp.concatenate(
        [g[:, :MLA_NOPE], _pair_layout(jnp.concatenate([g[:, MLA_NOPE:], g[:, MLA_NOPE:]], axis=-1))], axis=-1)
    w["gq"] = (pair_gain(mla_qn) * qs_mla).reshape(L, 1, 2 * LANES)
    w["gk"] = pair_gain(mla_kn).reshape(L, 1, 2 * LANES)
    ones = jnp.ones((L, AX_KV * AX_HD), F32)
    w["qkv_gain"] = jnp.concatenate(
        [jnp.tile(_pair_layout(ax_qn), (1, AX_HEADS)) * (AX_HD ** -0.5 * LOG2E),
         jnp.tile(_pair_layout(ax_kn), (1, AX_KV)), ones,
         jnp.tile(win_qn, (1, WIN_HEADS)) * (WIN_HD ** -0.5 * LOG2E), jnp.tile(win_kn, (1, WIN_KV)), ones,
         q_a_norm, kv_a_norm], axis=-1).reshape(L, 1, -1)
    w["br"] = [w_br_mla.astype(BF16), w_br_ax.astype(BF16), w_br_win.astype(BF16)]
    w["out"] = w_out.astype(BF16)
    return w


def _layer(x, h, mod, mod_next, l, ln_g, w, tabs, bias_tab, sink_tab):
    B, S, D = x.shape
    T = B * S
    h2 = h.reshape(T, D)
    cos_m, sin_m, cos_a, sin_a = tabs[S]
    qkv, kr2 = _qkv_proj(h2, w["qkv"][l], w["qkv_gain"][l], cos_a, sin_a, S)
    q_m, k_m, v_m = _mla_prep(qkv, 2 * LANES * len(QKV_KINDS), kr2, w["q_up"][l], w["kv_up"][l],
                              w["gq"][l], w["gk"][l], cos_m, sin_m, S)
    gate = _gate_proj(h2, w["gate"][l])
    r3 = lambda a: a.reshape(B, S, a.shape[-1])
    q_m, k_m, v_m, qkv3, gate3 = r3(q_m), r3(k_m), r3(v_m), r3(qkv), r3(gate)
    a_mla = _flash_attn(q_m, 0, k_m, 0, v_m, 0, gate3, 0,
                        B=B, S=S, H=MLA_HEADS, KV=MLA_HEADS, dk=2 * LANES, rows=min(1024, S))
    gq = AX_HEADS // AX_KV
    a_ax = _flash_attn(qkv3, 0, qkv3, AX_HEADS, qkv3, AX_HEADS + AX_KV, gate3, AX_KV,
                       B=B, S=S, H=AX_HEADS, KV=AX_KV, dk=AX_HD, rows=min(1024, S))
    wbase = AX_HEADS + 2 * AX_KV
    a_win = _window_attn(qkv3, wbase // gq, wbase + WIN_HEADS, wbase + WIN_HEADS + WIN_KV,
                         gate3, 2 * AX_KV, bias_tab, sink_tab[l], B=B, S=S)
    a_list = [a.reshape(T, a.shape[-1]) for a in (a_mla, a_ax, a_win)]
    y = _branch_merge(a_list, [wb[l] for wb in w["br"]], gate, 3 * MLA_HEADS * MLA_V)
    if mod_next is None:
        return _out_proj(y.reshape(B, S, D), w["out"][l], x, mod)
    return _out_proj(y.reshape(B, S, D), w["out"][l], x, mod, ln_g[l + 1], mod_next)


def kernel(x_prompt, x_sample, c_prompt, c_sample, ln_g, w_ada, b_ada, w_in, q_a_norm, w_q_up, kv_a_norm, w_kv_up, mla_qn, mla_kn, ax_qn, ax_kn, win_qn, win_kn, win_sink, rel_bias, w_br_mla, w_br_ax, w_br_win, w_out):
    L = ln_g.shape[0]
    D = x_prompt.shape[-1]
    Bp = x_prompt.shape[0]
    w = _prep_weights(D, w_in, w_q_up, w_kv_up, q_a_norm, kv_a_norm, mla_qn, mla_kn,
                      ax_qn, ax_kn, win_qn, win_kn, w_br_mla, w_br_ax, w_br_win, w_out)
    mod_all = _modulation(jnp.concatenate([c_prompt, c_sample], axis=0), w_ada, b_ada)
    tabs = {}
    for S in {x_prompt.shape[1], x_sample.shape[1]}:
        pos = jnp.arange(S, dtype=F32)
        row = jnp.repeat(jnp.arange(S // GRID_W, dtype=F32), GRID_W)
        colp = jnp.tile(jnp.arange(GRID_W, dtype=F32), S // GRID_W)
        tabs[S] = _rope_tables(pos, pos) + _rope_tables(row, colp)
    bias_tab = _window_bias_table(rel_bias)
    G = WIN_HEADS // WIN_KV
    sink_tab = jnp.repeat(win_sink.astype(F32) * LOG2E, BLOCK, axis=-1).reshape(L, WIN_KV, G * BLOCK, 1)
    mods_p = [mod_all[l, :Bp][:, None, :] for l in range(L)]
    mods_s = [mod_all[l, Bp:][:, None, :] for l in range(L)]
    yp, ys = x_prompt, x_sample
    hp = _norm_modulate(yp, ln_g[0], mods_p[0])
    hs = _norm_modulate(ys, ln_g[0], mods_s[0])
    for l in range(L):
        nxt = l + 1 < L
        yp, hp = _layer(yp, hp, mods_p[l], mods_p[l + 1] if nxt else None, l, ln_g, w, tabs, bias_tab, sink_tab)
        ys, hs = _layer(ys, hs, mods_s[l], mods_s[l + 1] if nxt else None, l, ln_g, w, tabs, bias_tab, sink_tab)
    return (yp, ys)
```

```python
import functools
import math

import jax
import jax.numpy as jnp
from jax import lax
from jax.experimental import pallas as pl
from jax.experimental.pallas import tpu as pltpu

F32 = jnp.float32
BF16 = jnp.bfloat16

GRID_W = 64
BLOCK = 128
EPS = 1e-6
ROPE_THETA = 10000.0
NEG = -1e30
LOG2E = 1.4426950408889634
LANES = 128

MLA_HEADS = 8
MLA_NOPE = 128
MLA_ROPE = 64
MLA_QK = MLA_NOPE + MLA_ROPE
MLA_V = 128
Q_LORA = 512
KV_LORA = 256
AX_HEADS = 8
AX_KV = 2
AX_HD = 128
WIN_HEADS = 8
WIN_KV = 2
WIN_HD = 128
WINDOW = 128
N_BUCKETS = 32
MAX_DIST = 128
ROPE_HALF = 32

VMEM_LIMIT = 56 * 1024 * 1024
GATE_COL0 = 4096
MLA_PREP_RC = 128


def _cparams(sem):
    return pltpu.CompilerParams(dimension_semantics=sem, vmem_limit_bytes=VMEM_LIMIT)


def _pick(n, pref):
    t = min(pref, n)
    while n % t:
        t //= 2
    return t


def _mod_kernel(c_ref, w_ref, b_ref, o_ref):
    o_ref[0] = jnp.dot(c_ref[...], w_ref[0].astype(BF16), preferred_element_type=F32) + b_ref[0]


def _modulation(c_all, w_ada, b_ada):
    L, D, N = w_ada.shape
    R = c_all.shape[0]
    tn = _pick(N, 512)
    return pl.pallas_call(
        _mod_kernel,
        grid=(L, N // tn),
        in_specs=[pl.BlockSpec((R, D), lambda l, j: (0, 0)),
                  pl.BlockSpec((1, D, tn), lambda l, j: (l, 0, j)),
                  pl.BlockSpec((1, 1, tn), lambda l, j: (l, 0, j))],
        out_specs=pl.BlockSpec((1, R, tn), lambda l, j: (l, 0, j)),
        out_shape=jax.ShapeDtypeStruct((L, R, N), F32),
        compiler_params=_cparams(("parallel", "parallel")),
        name="adaln_mod",
    )(c_all.astype(BF16), w_ada, b_ada.reshape(L, 1, N))


def _h_kernel(x_ref, g_ref, sh_ref, sc_ref, o_ref):
    x = x_ref[0]
    r = lax.rsqrt(jnp.mean(x * x, axis=-1, keepdims=True) + EPS)
    y = x * r * g_ref[...]
    o_ref[0] = (y * (1.0 + sc_ref[0]) + sh_ref[0]).astype(BF16)


def _norm_modulate(x, ln_g, mod):
    B, S, D = x.shape
    ts = _pick(S, 512)
    return pl.pallas_call(
        _h_kernel,
        grid=(B, S // ts),
        in_specs=[pl.BlockSpec((1, ts, D), lambda b, i: (b, i, 0)),
                  pl.BlockSpec((1, D), lambda b, i: (0, 0)),
                  pl.BlockSpec((1, 1, D), lambda b, i: (b, 0, 0)),
                  pl.BlockSpec((1, 1, D), lambda b, i: (b, 0, 1))],
        out_specs=pl.BlockSpec((1, ts, D), lambda b, i: (b, i, 0)),
        out_shape=jax.ShapeDtypeStruct((B, S, D), BF16),
        compiler_params=_cparams(("parallel", "parallel")),
        name="norm_modulate",
    )(x, ln_g.reshape(1, D), mod, mod)


def _pair_layout(x):
    shp = x.shape
    x = x.reshape(shp[:-1] + (shp[-1] // LANES, 2, 2, ROPE_HALF))
    return jnp.swapaxes(x, -3, -2).reshape(shp)


def _rope128(y, cos, sin_signed):
    return y * cos + pltpu.roll(y, LANES // 2, 1) * sin_signed


def _rope_tables(pos_a, pos_b):
    freqs = ROPE_THETA ** (-jnp.arange(ROPE_HALF, dtype=F32) / ROPE_HALF)
    ang_a = pos_a[:, None] * freqs[None, :]
    ang_b = pos_b[:, None] * freqs[None, :]
    cos = jnp.concatenate([jnp.cos(ang_a), jnp.cos(ang_b)], axis=-1)
    sin = jnp.concatenate([jnp.sin(ang_a), jnp.sin(ang_b)], axis=-1)
    return jnp.concatenate([cos, cos], axis=-1), jnp.concatenate([-sin, sin], axis=-1)


def _mla_prep_kernel(cqn_ref, ckvn_ref, kr_ref, wq_ref, wkv_ref, gq_ref, gk_ref,
                     cos_ref, sin_ref, q_ref, k_ref, v_ref, *, rc):
    for r in range(cqn_ref.shape[0] // rc):
        _mla_prep_rows(slice(rc * r, rc * (r + 1)), cqn_ref, ckvn_ref, kr_ref, wq_ref, wkv_ref,
                       gq_ref, gk_ref, cos_ref, sin_ref, q_ref, k_ref, v_ref)


def _mla_prep_rows(rows, cqn_ref, ckvn_ref, kr_ref, wq_ref, wkv_ref, gq_ref, gk_ref,
                   cos_ref, sin_ref, q_ref, k_ref, v_ref):
    kr2 = kr_ref[rows, :]
    qraw = jnp.dot(cqn_ref[rows, :], wq_ref[...], preferred_element_type=F32)
    kvraw = jnp.dot(ckvn_ref[rows, :], wkv_ref[...], preferred_element_type=F32)
    cos = cos_ref[rows, :]
    sin = sin_ref[rows, :]
    lane = lax.broadcasted_iota(jnp.int32, cos.shape, 1)
    even = (lane % (2 * ROPE_HALF)) < ROPE_HALF
    gq = gq_ref[...]
    gk = gk_ref[...]
    nope_w = MLA_HEADS * MLA_NOPE
    inv_qk = 1.0 / MLA_QK
    kr_sq = kr2 * kr2
    k_rope = _rope128(kr2 * gk[:, LANES:], cos, sin)
    for p in range(MLA_HEADS // 2):
        qr = qraw[:, nope_w + LANES * p: nope_w + LANES * (p + 1)]
        qr_sq = qr * qr
        q_rope = _rope128(qr * gq[:, LANES:], cos, sin)
        for e in range(2):
            hd = 2 * p + e
            keep = even if e == 0 else jnp.logical_not(even)
            qn = qraw[:, LANES * hd: LANES * (hd + 1)]
            rq = lax.rsqrt(jnp.sum(qn * qn + jnp.where(keep, qr_sq, 0.0), axis=-1, keepdims=True) * inv_qk + EPS)
            q_ref[rows, 2 * LANES * hd: 2 * LANES * hd + LANES] = (qn * rq * gq[:, :LANES]).astype(BF16)
            q_ref[rows, 2 * LANES * hd + LANES: 2 * LANES * (hd + 1)] = jnp.where(keep, q_rope * rq, 0.0).astype(BF16)
            kn = kvraw[:, LANES * hd: LANES * (hd + 1)]
            rk = lax.rsqrt(jnp.sum(kn * kn + jnp.where(keep, kr_sq, 0.0), axis=-1, keepdims=True) * inv_qk + EPS)
            k_ref[rows, 2 * LANES * hd: 2 * LANES * hd + LANES] = (kn * rk * gk[:, :LANES]).astype(BF16)
            k_ref[rows, 2 * LANES * hd + LANES: 2 * LANES * (hd + 1)] = jnp.where(keep, k_rope * rk, 0.0).astype(BF16)
    v_ref[rows, :] = kvraw[:, nope_w:].astype(BF16)


def _mla_prep(lat, lat_col0, kr2, wq, wkv, gq, gk, cos, sin, S):
    T = lat.shape[0]
    tm = _pick(S, 1024)
    nS = S // tm
    hq = MLA_HEADS * 2 * LANES
    const = lambda i: (0, 0)
    return pl.pallas_call(
        functools.partial(_mla_prep_kernel, rc=min(MLA_PREP_RC, tm)),
        grid=(T // tm,),
        in_specs=[pl.BlockSpec((tm, Q_LORA), lambda i: (i, lat_col0 // Q_LORA)),
                  pl.BlockSpec((tm, KV_LORA), lambda i: (i, (lat_col0 + Q_LORA) // KV_LORA)),
                  pl.BlockSpec((tm, LANES), lambda i: (i, 0)),
                  pl.BlockSpec(wq.shape, const),
                  pl.BlockSpec(wkv.shape, const),
                  pl.BlockSpec((1, 2 * LANES), const),
                  pl.BlockSpec((1, 2 * LANES), const),
                  pl.BlockSpec((tm, LANES), lambda i: (i % nS, 0)),
                  pl.BlockSpec((tm, LANES), lambda i: (i % nS, 0))],
        out_specs=[pl.BlockSpec((tm, hq), lambda i: (i, 0)),
                   pl.BlockSpec((tm, hq), lambda i: (i, 0)),
                   pl.BlockSpec((tm, MLA_HEADS * MLA_V), lambda i: (i, 0))],
        out_shape=[jax.ShapeDtypeStruct((T, hq), BF16),
                   jax.ShapeDtypeStruct((T, hq), BF16),
                   jax.ShapeDtypeStruct((T, MLA_HEADS * MLA_V), BF16)],
        compiler_params=_cparams(("parallel",)),
        name="mla_prep",
    )(lat, lat, kr2, wq, wkv, gq, gk, cos, sin)


QKV_KINDS = (0, 0, 0, 0, 0, 2, 1, 1, 1, 1, 1, 2)


def _qkv_kernel(h_ref, w_ref, gain_ref, cos_ref, sin_ref, o_ref, kr_ref, *, rc):
    tm = h_ref.shape[0]
    tn = 2 * LANES
    c0 = tn * len(QKV_KINDS)
    for r in range(tm // rc):
        rows = slice(rc * r, rc * (r + 1))
        for lo, width in ((c0, Q_LORA), (c0 + Q_LORA, KV_LORA)):
            x = jnp.dot(h_ref[rows, :], w_ref[:, lo: lo + width], preferred_element_type=F32)
            y = x * lax.rsqrt(jnp.mean(x * x, axis=-1, keepdims=True) + EPS) * gain_ref[:, lo: lo + width]
            o_ref[rows, lo: lo + width] = y.astype(BF16)
        kr_ref[rows, :] = jnp.dot(h_ref[rows, :], w_ref[:, c0 + Q_LORA + KV_LORA:], preferred_element_type=F32)
        for t, kind in enumerate(QKV_KINDS):
            acc = jnp.dot(h_ref[rows, :], w_ref[:, tn * t: tn * (t + 1)], preferred_element_type=F32)
            if kind == 2:
                o_ref[rows, tn * t: tn * (t + 1)] = acc.astype(BF16)
                continue
            for e in range(2):
                lanes = slice(tn * t + LANES * e, tn * t + LANES * (e + 1))
                x = acc[:, LANES * e: LANES * (e + 1)]
                y = x * lax.rsqrt(jnp.mean(x * x, axis=-1, keepdims=True) + EPS) * gain_ref[:, lanes]
                if kind == 0:
                    y = _rope128(y, cos_ref[rows, :], sin_ref[rows, :])
                o_ref[rows, lanes] = y.astype(BF16)


def _qkv_proj(h2, w, N, gains, cos, sin, S):
    T, D = h2.shape
    No = N - LANES
    tm = _pick(S, 512)
    nS = S // tm
    return pl.pallas_call(
        functools.partial(_qkv_kernel, rc=min(256, tm)),
        grid=(T // tm,),
        in_specs=[pl.BlockSpec((tm, D), lambda i: (i, 0)),
                  pl.BlockSpec((D, N), lambda i: (0, 0), pipeline_mode=pl.Buffered(1)),
                  pl.BlockSpec((1, No), lambda i: (0, 0)),
                  pl.BlockSpec((tm, LANES), lambda i: (i % nS, 0)),
                  pl.BlockSpec((tm, LANES), lambda i: (i % nS, 0))],
        out_specs=[pl.BlockSpec((tm, No), lambda i: (i, 0)),
                   pl.BlockSpec((tm, LANES), lambda i: (i, 0))],
        out_shape=[jax.ShapeDtypeStruct((T, No), BF16),
                   jax.ShapeDtypeStruct((T, LANES), F32)],
        compiler_params=_cparams(("parallel",)),
        name="qkv_proj",
    )(h2, w, gains, cos, sin)


def _gate_kernel(h_ref, w_ref, o_ref):
    o_ref[...] = jnp.dot(h_ref[...], w_ref[...], preferred_element_type=F32)


def _gate_proj(h2, w, N):
    T, D = h2.shape
    tm = _pick(T, 2048)
    tn = _pick(N, 512)
    j0 = GATE_COL0 // tn
    return pl.pallas_call(
        _gate_kernel,
        grid=(T // tm, N // tn),
        in_specs=[pl.BlockSpec((tm, D), lambda i, j: (i, 0)),
                  pl.BlockSpec((D, tn), lambda i, j: (0, j0 + j))],
        out_specs=pl.BlockSpec((tm, tn), lambda i, j: (i, j)),
        out_shape=jax.ShapeDtypeStruct((T, N), F32),
        compiler_params=_cparams(("parallel", "arbitrary")),
        name="gate_proj",
    )(h2, w)


def _flash_kernel(q_ref, k_ref, v_ref, g_ref, o_ref, vext_ref, *, G, dk, tq, tk, S, nt):
    qi = pl.program_id(2)

    def build_vext():
        lane = lax.broadcasted_iota(jnp.int32, (S, LANES), 1)
        vext_ref[:, :LANES] = v_ref[0]
        vext_ref[:, LANES:] = jnp.where(lane == 0, 1.0, 0.0).astype(BF16)

    if nt * tq == S:
        build_vext()
    else:
        pl.when(qi == 0)(build_vext)
    n = S // tk
    for t in range(nt):
        qrows = slice(tq * t, tq * (t + 1))
        if G == 1:
            q = q_ref[0, qrows, :]
        else:
            q = jnp.concatenate([q_ref[0, qrows, dk * g: dk * (g + 1)] for g in range(G)], axis=0)

        def scores(c, q=q):
            k = k_ref[0, tk * c: tk * (c + 1), :]
            return lax.dot_general(q, k, (((1,), (1,)), ((), ())), preferred_element_type=F32)

        s_next = scores(0)
        m = None
        acc = None
        for c in range(n):
            s = s_next
            if c + 1 < n:
                s_next = scores(c + 1)
            m_blk = jnp.max(s, axis=1, keepdims=True)
            m_new = m_blk if m is None else jnp.maximum(m, m_blk)
            p = jnp.exp2(s - m_new).astype(BF16)
            pv = jnp.dot(p, vext_ref[tk * c: tk * (c + 1), :], preferred_element_type=F32)
            acc = pv if acc is None else jnp.exp2(m - m_new) * acc + pv
            m = m_new
        o = acc[:, :LANES] / acc[:, LANES:LANES + 1]
        for g in range(G):
            gate = g_ref[0, qrows, LANES * g: LANES * (g + 1)]
            o_ref[0, qrows, LANES * g: LANES * (g + 1)] = (
                o[tq * g: tq * (g + 1)] * (gate * jax.nn.sigmoid(gate))).astype(BF16)


def _flash_attn(q_arr, q_blk0, k_arr, k_blk0, v_arr, v_blk0, g_arr, g_blk0, *, B, S, H, KV, dk, rows):
    G = H // KV
    tq = rows // G
    tk = _pick(S, 1024)
    nt = min(4, S // tq)
    kern = functools.partial(_flash_kernel, G=G, dk=dk, tq=tq, tk=tk, S=S, nt=nt)
    return pl.pallas_call(
        kern,
        grid=(B, KV, S // (tq * nt)),
        in_specs=[pl.BlockSpec((1, tq * nt, G * dk), lambda b, h, i: (b, i, q_blk0 + h)),
                  pl.BlockSpec((1, S, dk), lambda b, h, i: (b, 0, k_blk0 + h)),
                  pl.BlockSpec((1, S, LANES), lambda b, h, i: (b, 0, v_blk0 + h)),
                  pl.BlockSpec((1, tq * nt, G * LANES), lambda b, h, i: (b, i, g_blk0 + h))],
        out_specs=pl.BlockSpec((1, tq * nt, G * LANES), lambda b, h, i: (b, i, h)),
        out_shape=jax.ShapeDtypeStruct((B, S, H * LANES), BF16),
        scratch_shapes=[pltpu.VMEM((S, 2 * LANES), BF16)],
        compiler_params=_cparams(("parallel", "parallel", "arbitrary")),
        name=f"flash_attn_g{G}",
    )(q_arr, k_arr, v_arr, g_arr)


def _win_kernel(q_ref, kp_ref, kc_ref, kn_ref, vp_ref, vc_ref, vn_ref, bias_ref, sink_ref, g_ref, o_ref, *, G, nq, nsteps):
    j = pl.program_id(2)
    nt = (((1,), (1,)), ((), ()))
    sink = sink_ref[0]

    def kv_block(prev_ref, cur_ref, next_ref, t):
        if t < 0:
            return prev_ref[0]
        if t >= nq:
            return next_ref[0]
        return cur_ref[0, BLOCK * t: BLOCK * (t + 1), :]

    for t in range(nq):
        rows = slice(BLOCK * t, BLOCK * (t + 1))
        q = jnp.concatenate([q_ref[0, rows, LANES * g: LANES * (g + 1)] for g in range(G)], axis=0)
        s = [lax.dot_general(q, kv_block(kp_ref, kc_ref, kn_ref, t - 1 + d), nt, preferred_element_type=F32)
             + bias_ref[0, :, BLOCK * d: BLOCK * (d + 1)] for d in range(3)]
        if t == 0:
            s[0] = s[0] + jnp.where(j == 0, NEG, 0.0)
        if t == nq - 1:
            s[2] = s[2] + jnp.where(j == nsteps - 1, NEG, 0.0)
        m = jnp.maximum(jnp.max(jnp.maximum(jnp.maximum(s[0], s[1]), s[2]), axis=1, keepdims=True), sink)
        p = [jnp.exp2(sd - m) for sd in s]
        l = jnp.sum(p[0] + p[1] + p[2], axis=1, keepdims=True) + jnp.exp2(sink - m)
        o = sum(jnp.dot(p[d].astype(BF16), kv_block(vp_ref, vc_ref, vn_ref, t - 1 + d), preferred_element_type=F32)
                for d in range(3)) / l
        for g in range(G):
            gate = g_ref[0, rows, LANES * g: LANES * (g + 1)]
            o_ref[0, rows, LANES * g: LANES * (g + 1)] = (
                o[BLOCK * g: BLOCK * (g + 1)] * (gate * jax.nn.sigmoid(gate))).astype(BF16)


def _window_attn(qkv, q_blk0, k_blk0, v_blk0, gate, g_blk0, bias, sink, *, B, S):
    G = WIN_HEADS // WIN_KV
    nb = S // BLOCK
    nq = 8 if nb % 8 == 0 else (4 if nb % 4 == 0 else 1)
    nsteps = nb // nq
    kern = functools.partial(_win_kernel, G=G, nq=nq, nsteps=nsteps)

    def kv_specs(blk0):
        return [pl.BlockSpec((1, BLOCK, LANES), lambda b, h, j: (b, jnp.maximum(j * nq - 1, 0), blk0 + h)),
                pl.BlockSpec((1, nq * BLOCK, LANES), lambda b, h, j: (b, j, blk0 + h)),
                pl.BlockSpec((1, BLOCK, LANES), lambda b, h, j: (b, jnp.minimum((j + 1) * nq, nb - 1), blk0 + h))]

    return pl.pallas_call(
        kern,
        grid=(B, WIN_KV, nsteps),
        in_specs=[pl.BlockSpec((1, nq * BLOCK, G * LANES), lambda b, h, j: (b, j, q_blk0 + h))]
        + kv_specs(k_blk0) + kv_specs(v_blk0)
        + [pl.BlockSpec((1, G * BLOCK, 3 * BLOCK), lambda b, h, j: (h, 0, 0)),
           pl.BlockSpec((1, G * BLOCK, 1), lambda b, h, j: (h, 0, 0)),
           pl.BlockSpec((1, nq * BLOCK, G * LANES), lambda b, h, j: (b, j, g_blk0 + h))],
        out_specs=pl.BlockSpec((1, nq * BLOCK, G * LANES), lambda b, h, j: (b, j, h)),
        out_shape=jax.ShapeDtypeStruct((B, S, WIN_HEADS * LANES), BF16),
        compiler_params=_cparams(("parallel", "parallel", "arbitrary")),
        name="window_attn",
    )(qkv, qkv, qkv, qkv, qkv, qkv, qkv, bias, sink, gate)


def _t5_bucket(rel):
    half = N_BUCKETS // 2
    max_exact = half // 2
    ret = jnp.where(rel > 0, half, 0)
    n = jnp.abs(rel)
    nf = jnp.maximum(n, 1).astype(F32)
    large = max_exact + (jnp.log(nf / max_exact) / math.log(MAX_DIST / max_exact)
                         * (half - max_exact)).astype(jnp.int32)
    large = jnp.minimum(large, half - 1)
    return ret + jnp.where(n < max_exact, n, large)


def _bias_kernel(bucket_ref, rb_ref, o_ref):
    bucket = bucket_ref[...]
    row = lax.broadcasted_iota(jnp.int32, bucket.shape, 0)
    colk = lax.broadcasted_iota(jnp.int32, bucket.shape, 1)
    inband = jnp.abs(colk - WINDOW - row) <= WINDOW
    for h in range(WIN_HEADS):
        acc = jnp.zeros(bucket.shape, F32)
        for b in range(N_BUCKETS):
            acc = jnp.where(bucket == b, rb_ref[b, h], acc)
        o_ref[h] = jnp.where(inband, acc * LOG2E, NEG)


def _window_bias_table(rel_bias):
    span = BLOCK + 2 * WINDOW
    rel = jnp.arange(span)[None, :] - WINDOW - jnp.arange(BLOCK)[:, None]
    bias = pl.pallas_call(
        _bias_kernel,
        in_specs=[pl.BlockSpec(memory_space=pltpu.VMEM), pl.BlockSpec(memory_space=pltpu.SMEM)],
        out_specs=pl.BlockSpec(memory_space=pltpu.VMEM),
        out_shape=jax.ShapeDtypeStruct((WIN_HEADS, BLOCK, span), F32),
        name="t5_bias_table",
    )(_t5_bucket(rel).astype(jnp.int32), rel_bias.astype(F32))
    G = WIN_HEADS // WIN_KV
    return bias.reshape(WIN_KV, G * BLOCK, span)


def _branch_kernel(a1, a2, a3, w1, w2, w3, m1, m2, m3, o_ref):
    y = jax.nn.sigmoid(m1[...]) * jnp.dot(a1[...], w1[...], preferred_element_type=F32)
    y = y + jax.nn.sigmoid(m2[...]) * jnp.dot(a2[...], w2[...], preferred_element_type=F32)
    y = y + jax.nn.sigmoid(m3[...]) * jnp.dot(a3[...], w3[...], preferred_element_type=F32)
    o_ref[...] = y.astype(BF16)


def _branch_merge(a_list, w_list, gate, m_col0):
    T, W = a_list[0].shape
    D = w_list[0].shape[1]
    tm = _pick(T, 512)
    tn = _pick(D, 1024)
    nj = D // tn
    a_spec = pl.BlockSpec((tm, W), lambda j, i: (i, 0))
    w_spec = pl.BlockSpec((W, tn), lambda j, i: (0, j))

    def m_spec(k):
        base = (m_col0 + k * D) // tn
        return pl.BlockSpec((tm, tn), lambda j, i: (i, base + j))

    return pl.pallas_call(
        _branch_kernel,
        grid=(nj, T // tm),
        in_specs=[a_spec] * 3 + [w_spec] * 3 + [m_spec(0), m_spec(1), m_spec(2)],
        out_specs=pl.BlockSpec((tm, tn), lambda j, i: (i, j)),
        out_shape=jax.ShapeDtypeStruct((T, D), BF16),
        compiler_params=_cparams(("parallel", "arbitrary")),
        name="branch_merge",
    )(*a_list, *w_list, gate, gate, gate)


def _out_kernel(y_ref, w_ref, x_ref, gt_ref, *rest, rc, with_next):
    if with_next:
        g_ref, sh_ref, sc_ref, o_ref, hn_ref = rest
    else:
        (o_ref,) = rest
    for r in range(y_ref.shape[1] // rc):
        rows = slice(rc * r, rc * (r + 1))
        x = x_ref[0, rows, :] + gt_ref[0] * jnp.dot(y_ref[0, rows, :], w_ref[...], preferred_element_type=F32)
        o_ref[0, rows, :] = x
        if with_next:
            rms = lax.rsqrt(jnp.mean(x * x, axis=-1, keepdims=True) + EPS)
            hn_ref[0, rows, :] = ((x * rms * g_ref[...]) * (1.0 + sc_ref[0]) + sh_ref[0]).astype(BF16)


def _out_proj(y, w_out, x, mod, ln_g_next=None, mod_next=None):
    B, S, D = x.shape
    tm = _pick(S, 512)
    with_next = ln_g_next is not None
    row_spec = pl.BlockSpec((1, tm, D), lambda b, i: (b, i, 0))
    in_specs = [row_spec,
                pl.BlockSpec((D, D), lambda b, i: (0, 0)),
                row_spec,
                pl.BlockSpec((1, 1, D), lambda b, i: (b, 0, 2))]
    args = [y, w_out, x, mod]
    out_specs = [row_spec]
    out_shape = [jax.ShapeDtypeStruct((B, S, D), F32)]
    if with_next:
        in_specs += [pl.BlockSpec((1, D), lambda b, i: (0, 0)),
                     pl.BlockSpec((1, 1, D), lambda b, i: (b, 0, 0)),
                     pl.BlockSpec((1, 1, D), lambda b, i: (b, 0, 1))]
        args += [ln_g_next.reshape(1, D), mod_next, mod_next]
        out_specs.append(row_spec)
        out_shape.append(jax.ShapeDtypeStruct((B, S, D), BF16))
    res = pl.pallas_call(
        functools.partial(_out_kernel, rc=min(128, tm), with_next=with_next),
        grid=(B, S // tm),
        in_specs=in_specs,
        out_specs=out_specs,
        out_shape=out_shape,
        compiler_params=_cparams(("parallel", "parallel")),
        name="out_proj",
    )(*args)
    return (res[0], res[1]) if with_next else (res[0], None)


def _split_points(D):
    sizes = [Q_LORA, KV_LORA, MLA_ROPE, MLA_HEADS * MLA_V,
             AX_HEADS * AX_HD, AX_KV * AX_HD, AX_KV * AX_HD, AX_HEADS * AX_HD,
             WIN_HEADS * WIN_HD, WIN_KV * WIN_HD, WIN_KV * WIN_HD, WIN_HEADS * WIN_HD,
             D, D, D]
    pts, acc = [0], 0
    for s in sizes:
        acc += s
        pts.append(acc)
    return pts


def _prep_weights(D, w_in, w_q_up, w_kv_up, q_a_norm, kv_a_norm, mla_qn, mla_kn,
                  ax_qn, ax_kn, win_qn, win_kn, w_br_mla, w_br_ax, w_br_win, w_out):
    p = _split_points(D)
    col = lambda k: w_in[:, :, p[k]:p[k + 1]]
    (cq, ckv, kr, g_mla, aq, ak, av, g_ax, wq, wk, wv, g_win, m_mla, m_ax, m_win) = [col(k) for k in range(15)]
    L = w_in.shape[0]
    w = {}
    qkv_cols = [_pair_layout(aq), _pair_layout(ak), av, wq, wk, wv, cq, ckv,
                _pair_layout(jnp.concatenate([kr, kr], axis=-1))]
    gate_cols = [g_mla, g_ax, g_win, m_mla, m_ax, m_win]
    n_qkv = sum(c.shape[-1] for c in qkv_cols)
    pad = jnp.zeros(w_in.shape[:2] + (GATE_COL0 - n_qkv,), w_in.dtype)
    w["in"] = jnp.concatenate(qkv_cols + [pad] + gate_cols, axis=-1).astype(BF16)
    w["n_qkv"] = n_qkv
    w["n_gate"] = sum(c.shape[-1] for c in gate_cols)
    wq4 = w_q_up.reshape(L, Q_LORA, MLA_HEADS, MLA_QK)
    w["q_up"] = jnp.concatenate([wq4[..., :MLA_NOPE].reshape(L, Q_LORA, -1),
                                 _pair_layout(wq4[..., MLA_NOPE:].reshape(L, Q_LORA, -1))], axis=-1).astype(BF16)
    wkv4 = w_kv_up.reshape(L, KV_LORA, MLA_HEADS, MLA_NOPE + MLA_V)
    w["kv_up"] = jnp.concatenate([wkv4[..., :MLA_NOPE].reshape(L, KV_LORA, -1),
                                  wkv4[..., MLA_NOPE:].reshape(L, KV_LORA, -1)], axis=-1).astype(BF16)
    qs_mla = MLA_QK ** -0.5 * LOG2E
    pair_gain = lambda g: jnp.concatenate(
        [g[:, :MLA_NOPE], _pair_layout(jnp.concatenate([g[:, MLA_NOPE:], g[:, MLA_NOPE:]], axis=-1))], axis=-1)
    w["gq"] = (pair_gain(mla_qn) * qs_mla).reshape(L, 1, 2 * LANES)
    w["gk"] = pair_gain(mla_kn).reshape(L, 1, 2 * LANES)
    ones = jnp.ones((L, AX_KV * AX_HD), F32)
    w["qkv_gain"] = jnp.concatenate(
        [jnp.tile(_pair_layout(ax_qn), (1, AX_HEADS)) * (AX_HD ** -0.5 * LOG2E),
         jnp.tile(_pair_layout(ax_kn), (1, AX_KV)), ones,
         jnp.tile(win_qn, (1, WIN_HEADS)) * (WIN_HD ** -0.5 * LOG2E), jnp.tile(win_kn, (1, WIN_KV)), ones,
         q_a_norm, kv_a_norm], axis=-1).reshape(L, 1, -1)
    w["br"] = [w_br_mla.astype(BF16), w_br_ax.astype(BF16), w_br_win.astype(BF16)]
    w["out"] = w_out.astype(BF16)
    return w


def _layer(x, h, mod, mod_next, l, ln_g, w, tabs, bias_tab, sink_tab):
    B, S, D = x.shape
    T = B * S
    h2 = h.reshape(T, D)
    cos_m, sin_m, cos_a, sin_a = tabs[S]
    qkv, kr2 = _qkv_proj(h2, w["in"][l], w["n_qkv"], w["qkv_gain"][l], cos_a, sin_a, S)
    q_m, k_m, v_m = _mla_prep(qkv, 2 * LANES * len(QKV_KINDS), kr2, w["q_up"][l], w["kv_up"][l],
                              w["gq"][l], w["gk"][l], cos_m, sin_m, S)
    gate = _gate_proj(h2, w["in"][l], w["n_gate"])
    r3 = lambda a: a.reshape(B, S, a.shape[-1])
    q_m, k_m, v_m, qkv3, gate3 = r3(q_m), r3(k_m), r3(v_m), r3(qkv), r3(gate)
    a_mla = _flash_attn(q_m, 0, k_m, 0, v_m, 0, gate3, 0,
                        B=B, S=S, H=MLA_HEADS, KV=MLA_HEADS, dk=2 * LANES, rows=min(1024, S))
    gq = AX_HEADS // AX_KV
    a_ax = _flash_attn(qkv3, 0, qkv3, AX_HEADS, qkv3, AX_HEADS + AX_KV, gate3, AX_KV,
                       B=B, S=S, H=AX_HEADS, KV=AX_KV, dk=AX_HD, rows=min(1024, S))
    wbase = AX_HEADS + 2 * AX_KV
    a_win = _window_attn(qkv3, wbase // gq, wbase + WIN_HEADS, wbase + WIN_HEADS + WIN_KV,
                         gate3, 2 * AX_KV, bias_tab, sink_tab[l], B=B, S=S)
    a_list = [a.reshape(T, a.shape[-1]) for a in (a_mla, a_ax, a_win)]
    y = _branch_merge(a_list, [wb[l] for wb in w["br"]], gate, 3 * MLA_HEADS * MLA_V)
    if mod_next is None:
        return _out_proj(y.reshape(B, S, D), w["out"][l], x, mod)
    return _out_proj(y.reshape(B, S, D), w["out"][l], x, mod, ln_g[l + 1], mod_next)


def kernel(x_prompt, x_sample, c_prompt, c_sample, ln_g, w_ada, b_ada, w_in, q_a_norm, w_q_up, kv_a_norm, w_kv_up, mla_qn, mla_kn, ax_qn, ax_kn, win_qn, win_kn, win_sink, rel_bias, w_br_mla, w_br_ax, w_br_win, w_out):
    L = ln_g.shape[0]
    D = x_prompt.shape[-1]
    Bp = x_prompt.shape[0]
    w = _prep_weights(D, w_in, w_q_up, w_kv_up, q_a_norm, kv_a_norm, mla_qn, mla_kn,
                      ax_qn, ax_kn, win_qn, win_kn, w_br_mla, w_br_ax, w_br_win, w_out)
    mod_all = _modulation(jnp.concatenate([c_prompt, c_sample], axis=0), w_ada, b_ada)
    tabs = {}
    for S in {x_prompt.shape[1], x_sample.shape[1]}:
        pos = jnp.arange(S, dtype=F32)
        row = jnp.repeat(jnp.arange(S // GRID_W, dtype=F32), GRID_W)
        colp = jnp.tile(jnp.arange(GRID_W, dtype=F32), S // GRID_W)
        tabs[S] = _rope_tables(pos, pos) + _rope_tables(row, colp)
    bias_tab = _window_bias_table(rel_bias)
    G = WIN_HEADS // WIN_KV
    sink_tab = jnp.repeat(win_sink.astype(F32) * LOG2E, BLOCK, axis=-1).reshape(L, WIN_KV, G * BLOCK, 1)
    mods_p = [mod_all[l, :Bp][:, None, :] for l in range(L)]
    mods_s = [mod_all[l, Bp:][:, None, :] for l in range(L)]
    yp, ys = x_prompt, x_sample
    hp = _norm_modulate(yp, ln_g[0], mods_p[0])
    hs = _norm_modulate(ys, ln_g[0], mods_s[0])
    for l in range(L):
        nxt = l + 1 < L
        yp, hp = _layer(yp, hp, mods_p[l], mods_p[l + 1] if nxt else None, l, ln_g, w, tabs, bias_tab, sink_tab)
        ys, hs = _layer(ys, hs, mods_s[l], mods_s[l + 1] if nxt else None, l, ln_g, w, tabs, bias_tab, sink_tab)
    return (yp, ys)
```

```python
import functools
import math

import jax
import jax.numpy as jnp
from jax import lax
from jax.experimental import pallas as pl
from jax.experimental.pallas import tpu as pltpu

F32 = jnp.float32
BF16 = jnp.bfloat16

GRID_W = 64
BLOCK = 128
EPS = 1e-6
ROPE_THETA = 10000.0
NEG = -1e30
LOG2E = 1.4426950408889634
LANES = 128

MLA_HEADS = 8
MLA_NOPE = 128
MLA_ROPE = 64
MLA_QK = MLA_NOPE + MLA_ROPE
MLA_V = 128
Q_LORA = 512
KV_LORA = 256
AX_HEADS = 8
AX_KV = 2
AX_HD = 128
WIN_HEADS = 8
WIN_KV = 2
WIN_HD = 128
WINDOW = 128
N_BUCKETS = 32
MAX_DIST = 128
ROPE_HALF = 32

VMEM_LIMIT = 56 * 1024 * 1024
GATE_COL0 = 4096
MLA_PREP_RC = 128


def _cparams(sem):
    return pltpu.CompilerParams(dimension_semantics=sem, vmem_limit_bytes=VMEM_LIMIT)


def _pick(n, pref):
    t = min(pref, n)
    while n % t:
        t //= 2
    return t


def _mod_kernel(c_ref, w_ref, b_ref, o_ref):
    o_ref[0] = jnp.dot(c_ref[...], w_ref[0].astype(BF16), preferred_element_type=F32) + b_ref[0]


def _modulation(c_all, w_ada, b_ada):
    L, D, N = w_ada.shape
    R = c_all.shape[0]
    tn = _pick(N, 512)
    return pl.pallas_call(
        _mod_kernel,
        grid=(L, N // tn),
        in_specs=[pl.BlockSpec((R, D), lambda l, j: (0, 0)),
                  pl.BlockSpec((1, D, tn), lambda l, j: (l, 0, j)),
                  pl.BlockSpec((1, 1, tn), lambda l, j: (l, 0, j))],
        out_specs=pl.BlockSpec((1, R, tn), lambda l, j: (l, 0, j)),
        out_shape=jax.ShapeDtypeStruct((L, R, N), F32),
        compiler_params=_cparams(("parallel", "parallel")),
        name="adaln_mod",
    )(c_all.astype(BF16), w_ada, b_ada.reshape(L, 1, N))


def _h_kernel(x_ref, g_ref, sh_ref, sc_ref, o_ref):
    x = x_ref[0]
    r = lax.rsqrt(jnp.mean(x * x, axis=-1, keepdims=True) + EPS)
    y = x * r * g_ref[...]
    o_ref[0] = (y * (1.0 + sc_ref[0]) + sh_ref[0]).astype(BF16)


def _norm_modulate(x, ln_g, mod):
    B, S, D = x.shape
    ts = _pick(S, 512)
    return pl.pallas_call(
        _h_kernel,
        grid=(B, S // ts),
        in_specs=[pl.BlockSpec((1, ts, D), lambda b, i: (b, i, 0)),
                  pl.BlockSpec((1, D), lambda b, i: (0, 0)),
                  pl.BlockSpec((1, 1, D), lambda b, i: (b, 0, 0)),
                  pl.BlockSpec((1, 1, D), lambda b, i: (b, 0, 1))],
        out_specs=pl.BlockSpec((1, ts, D), lambda b, i: (b, i, 0)),
        out_shape=jax.ShapeDtypeStruct((B, S, D), BF16),
        compiler_params=_cparams(("parallel", "parallel")),
        name="norm_modulate",
    )(x, ln_g.reshape(1, D), mod, mod)


def _pair_layout(x):
    shp = x.shape
    x = x.reshape(shp[:-1] + (shp[-1] // LANES, 2, 2, ROPE_HALF))
    return jnp.swapaxes(x, -3, -2).reshape(shp)


def _rope128(y, cos, sin_signed):
    return y * cos + pltpu.roll(y, LANES // 2, 1) * sin_signed


def _rope_tables(pos_a, pos_b):
    freqs = ROPE_THETA ** (-jnp.arange(ROPE_HALF, dtype=F32) / ROPE_HALF)
    ang_a = pos_a[:, None] * freqs[None, :]
    ang_b = pos_b[:, None] * freqs[None, :]
    cos = jnp.concatenate([jnp.cos(ang_a), jnp.cos(ang_b)], axis=-1)
    sin = jnp.concatenate([jnp.sin(ang_a), jnp.sin(ang_b)], axis=-1)
    return jnp.concatenate([cos, cos], axis=-1), jnp.concatenate([-sin, sin], axis=-1)


def _mla_prep_kernel(cqn_ref, ckvn_ref, kr_ref, wq_ref, wkv_ref, gq_ref, gk_ref,
                     cos_ref, sin_ref, q_ref, k_ref, v_ref, *, rc):
    for r in range(cqn_ref.shape[0] // rc):
        _mla_prep_rows(slice(rc * r, rc * (r + 1)), cqn_ref, ckvn_ref, kr_ref, wq_ref, wkv_ref,
                       gq_ref, gk_ref, cos_ref, sin_ref, q_ref, k_ref, v_ref)


def _mla_prep_rows(rows, cqn_ref, ckvn_ref, kr_ref, wq_ref, wkv_ref, gq_ref, gk_ref,
                   cos_ref, sin_ref, q_ref, k_ref, v_ref):
    kr2 = kr_ref[rows, :]
    qraw = jnp.dot(cqn_ref[rows, :], wq_ref[...], preferred_element_type=F32)
    kvraw = jnp.dot(ckvn_ref[rows, :], wkv_ref[...], preferred_element_type=F32)
    cos = cos_ref[rows, :]
    sin = sin_ref[rows, :]
    lane = lax.broadcasted_iota(jnp.int32, cos.shape, 1)
    even = (lane % (2 * ROPE_HALF)) < ROPE_HALF
    gq = gq_ref[...]
    gk = gk_ref[...]
    nope_w = MLA_HEADS * MLA_NOPE
    inv_qk = 1.0 / MLA_QK
    kr_sq = kr2 * kr2
    k_rope = _rope128(kr2 * gk[:, LANES:], cos, sin)
    for p in range(MLA_HEADS // 2):
        qr = qraw[:, nope_w + LANES * p: nope_w + LANES * (p + 1)]
        qr_sq = qr * qr
        q_rope = _rope128(qr * gq[:, LANES:], cos, sin)
        for e in range(2):
            hd = 2 * p + e
            keep = even if e == 0 else jnp.logical_not(even)
            qn = qraw[:, LANES * hd: LANES * (hd + 1)]
            rq = lax.rsqrt(jnp.sum(qn * qn + jnp.where(keep, qr_sq, 0.0), axis=-1, keepdims=True) * inv_qk + EPS)
            q_ref[rows, 2 * LANES * hd: 2 * LANES * hd + LANES] = (qn * rq * gq[:, :LANES]).astype(BF16)
            q_ref[rows, 2 * LANES * hd + LANES: 2 * LANES * (hd + 1)] = jnp.where(keep, q_rope * rq, 0.0).astype(BF16)
            kn = kvraw[:, LANES * hd: LANES * (hd + 1)]
            rk = lax.rsqrt(jnp.sum(kn * kn + jnp.where(keep, kr_sq, 0.0), axis=-1, keepdims=True) * inv_qk + EPS)
            k_ref[rows, 2 * LANES * hd: 2 * LANES * hd + LANES] = (kn * rk * gk[:, :LANES]).astype(BF16)
            k_ref[rows, 2 * LANES * hd + LANES: 2 * LANES * (hd + 1)] = jnp.where(keep, k_rope * rk, 0.0).astype(BF16)
    v_ref[rows, :] = kvraw[:, nope_w:].astype(BF16)


def _mla_prep(lat, lat_col0, kr2, wq, wkv, gq, gk, cos, sin, S):
    T = lat.shape[0]
    tm = _pick(S, 1024)
    nS = S // tm
    hq = MLA_HEADS * 2 * LANES
    const = lambda i: (0, 0)
    return pl.pallas_call(
        functools.partial(_mla_prep_kernel, rc=min(MLA_PREP_RC, tm)),
        grid=(T // tm,),
        in_specs=[pl.BlockSpec((tm, Q_LORA), lambda i: (i, lat_col0 // Q_LORA)),
                  pl.BlockSpec((tm, KV_LORA), lambda i: (i, (lat_col0 + Q_LORA) // KV_LORA)),
                  pl.BlockSpec((tm, LANES), lambda i: (i, 0)),
                  pl.BlockSpec(wq.shape, const),
                  pl.BlockSpec(wkv.shape, const),
                  pl.BlockSpec((1, 2 * LANES), const),
                  pl.BlockSpec((1, 2 * LANES), const),
                  pl.BlockSpec((tm, LANES), lambda i: (i % nS, 0)),
                  pl.BlockSpec((tm, LANES), lambda i: (i % nS, 0))],
        out_specs=[pl.BlockSpec((tm, hq), lambda i: (i, 0)),
                   pl.BlockSpec((tm, hq), lambda i: (i, 0)),
                   pl.BlockSpec((tm, MLA_HEADS * MLA_V), lambda i: (i, 0))],
        out_shape=[jax.ShapeDtypeStruct((T, hq), BF16),
                   jax.ShapeDtypeStruct((T, hq), BF16),
                   jax.ShapeDtypeStruct((T, MLA_HEADS * MLA_V), BF16)],
        compiler_params=_cparams(("parallel",)),
        name="mla_prep",
    )(lat, lat, kr2, wq, wkv, gq, gk, cos, sin)


QKV_KINDS = (0, 0, 0, 0, 0, 2, 1, 1, 1, 1, 1, 2)


def _qkv_kernel(h_ref, w_ref, gain_ref, cos_ref, sin_ref, o_ref, kr_ref, *, rc):
    tm = h_ref.shape[0]
    tn = 2 * LANES
    c0 = tn * len(QKV_KINDS)
    for r in range(tm // rc):
        rows = slice(rc * r, rc * (r + 1))
        for lo, width in ((c0, Q_LORA), (c0 + Q_LORA, KV_LORA)):
            x = jnp.dot(h_ref[rows, :], w_ref[:, lo: lo + width], preferred_element_type=F32)
            y = x * lax.rsqrt(jnp.mean(x * x, axis=-1, keepdims=True) + EPS) * gain_ref[:, lo: lo + width]
            o_ref[rows, lo: lo + width] = y.astype(BF16)
        kr_ref[rows, :] = jnp.dot(h_ref[rows, :], w_ref[:, c0 + Q_LORA + KV_LORA:], preferred_element_type=F32)
        for t, kind in enumerate(QKV_KINDS):
            acc = jnp.dot(h_ref[rows, :], w_ref[:, tn * t: tn * (t + 1)], preferred_element_type=F32)
            if kind == 2:
                o_ref[rows, tn * t: tn * (t + 1)] = acc.astype(BF16)
                continue
            for e in range(2):
                lanes = slice(tn * t + LANES * e, tn * t + LANES * (e + 1))
                x = acc[:, LANES * e: LANES * (e + 1)]
                y = x * lax.rsqrt(jnp.mean(x * x, axis=-1, keepdims=True) + EPS) * gain_ref[:, lanes]
                if kind == 0:
                    y = _rope128(y, cos_ref[rows, :], sin_ref[rows, :])
                o_ref[rows, lanes] = y.astype(BF16)


def _qkv_proj(h2, w, N, gains, cos, sin, S):
    T, D = h2.shape
    No = N - LANES
    tm = _pick(S, 512)
    nS = S // tm
    return pl.pallas_call(
        functools.partial(_qkv_kernel, rc=min(256, tm)),
        grid=(T // tm,),
        in_specs=[pl.BlockSpec((tm, D), lambda i: (i, 0)),
                  pl.BlockSpec((D, N), lambda i: (0, 0), pipeline_mode=pl.Buffered(1)),
                  pl.BlockSpec((1, No), lambda i: (0, 0)),
                  pl.BlockSpec((tm, LANES), lambda i: (i % nS, 0)),
                  pl.BlockSpec((tm, LANES), lambda i: (i % nS, 0))],
        out_specs=[pl.BlockSpec((tm, No), lambda i: (i, 0)),
                   pl.BlockSpec((tm, LANES), lambda i: (i, 0))],
        out_shape=[jax.ShapeDtypeStruct((T, No), BF16),
                   jax.ShapeDtypeStruct((T, LANES), F32)],
        compiler_params=_cparams(("parallel",)),
        name="qkv_proj",
    )(h2, w, gains, cos, sin)


def _gate_kernel(h_ref, w_ref, o_ref):
    o_ref[...] = jnp.dot(h_ref[...], w_ref[...], preferred_element_type=F32)


def _gate_proj(h2, w, N):
    T, D = h2.shape
    tm = _pick(T, 2048)
    tn = _pick(N, 512)
    j0 = GATE_COL0 // tn
    return pl.pallas_call(
        _gate_kernel,
        grid=(T // tm, N // tn),
        in_specs=[pl.BlockSpec((tm, D), lambda i, j: (i, 0)),
                  pl.BlockSpec((D, tn), lambda i, j: (0, j0 + j))],
        out_specs=pl.BlockSpec((tm, tn), lambda i, j: (i, j)),
        out_shape=jax.ShapeDtypeStruct((T, N), F32),
        compiler_params=_cparams(("parallel", "arbitrary")),
        name="gate_proj",
    )(h2, w)


def _flash_kernel(q_ref, k_ref, v_ref, g_ref, o_ref, vext_ref, *, G, dk, tq, tk, S, nt):
    qi = pl.program_id(2)

    def build_vext():
        lane = lax.broadcasted_iota(jnp.int32, (S, LANES), 1)
        vext_ref[:, :LANES] = v_ref[0]
        vext_ref[:, LANES:] = jnp.where(lane == 0, 1.0, 0.0).astype(BF16)

    if nt * tq == S:
        build_vext()
    else:
        pl.when(qi == 0)(build_vext)
    n = S // tk
    for t in range(nt):
        qrows = slice(tq * t, tq * (t + 1))
        if G == 1:
            q = q_ref[0, qrows, :]
        else:
            q = jnp.concatenate([q_ref[0, qrows, dk * g: dk * (g + 1)] for g in range(G)], axis=0)

        def scores(c, q=q):
            k = k_ref[0, tk * c: tk * (c + 1), :]
            return lax.dot_general(q, k, (((1,), (1,)), ((), ())), preferred_element_type=F32)

        s_next = scores(0)
        m = None
        acc = None
        for c in range(n):
            s = s_next
            if c + 1 < n:
                s_next = scores(c + 1)
            m_blk = jnp.max(s, axis=1, keepdims=True)
            m_new = m_blk if m is None else jnp.maximum(m, m_blk)
            p = jnp.exp2(s - m_new).astype(BF16)
            pv = jnp.dot(p, vext_ref[tk * c: tk * (c + 1), :], preferred_element_type=F32)
            acc = pv if acc is None else jnp.exp2(m - m_new) * acc + pv
            m = m_new
        o = acc[:, :LANES] / acc[:, LANES:LANES + 1]
        for g in range(G):
            gate = g_ref[0, qrows, LANES * g: LANES * (g + 1)]
            o_ref[0, qrows, LANES * g: LANES * (g + 1)] = (
                o[tq * g: tq * (g + 1)] * (gate * jax.nn.sigmoid(gate))).astype(BF16)


def _flash_attn(q_arr, q_blk0, k_arr, k_blk0, v_arr, v_blk0, g_arr, g_blk0, *, B, S, H, KV, dk, rows):
    G = H // KV
    tq = rows // G
    tk = _pick(S, 1024)
    nt = min(4, S // tq)
    kern = functools.partial(_flash_kernel, G=G, dk=dk, tq=tq, tk=tk, S=S, nt=nt)
    return pl.pallas_call(
        kern,
        grid=(B, KV, S // (tq * nt)),
        in_specs=[pl.BlockSpec((1, tq * nt, G * dk), lambda b, h, i: (b, i, q_blk0 + h)),
                  pl.BlockSpec((1, S, dk), lambda b, h, i: (b, 0, k_blk0 + h)),
                  pl.BlockSpec((1, S, LANES), lambda b, h, i: (b, 0, v_blk0 + h)),
                  pl.BlockSpec((1, tq * nt, G * LANES), lambda b, h, i: (b, i, g_blk0 + h))],
        out_specs=pl.BlockSpec((1, tq * nt, G * LANES), lambda b, h, i: (b, i, h)),
        out_shape=jax.ShapeDtypeStruct((B, S, H * LANES), BF16),
        scratch_shapes=[pltpu.VMEM((S, 2 * LANES), BF16)],
        compiler_params=_cparams(("parallel", "parallel", "arbitrary")),
        name=f"flash_attn_g{G}",
    )(q_arr, k_arr, v_arr, g_arr)


def _win_kernel(q_ref, kp_ref, kc_ref, kn_ref, vp_ref, vc_ref, vn_ref, bias_ref, sink_ref, g_ref, o_ref, *, G, nq, nsteps):
    j = pl.program_id(2)
    nt = (((1,), (1,)), ((), ()))
    sink = sink_ref[0]

    def kv_block(prev_ref, cur_ref, next_ref, t):
        if t < 0:
            return prev_ref[0]
        if t >= nq:
            return next_ref[0]
        return cur_ref[0, BLOCK * t: BLOCK * (t + 1), :]

    for t in range(nq):
        rows = slice(BLOCK * t, BLOCK * (t + 1))
        q = jnp.concatenate([q_ref[0, rows, LANES * g: LANES * (g + 1)] for g in range(G)], axis=0)
        s = [lax.dot_general(q, kv_block(kp_ref, kc_ref, kn_ref, t - 1 + d), nt, preferred_element_type=F32)
             + bias_ref[0, :, BLOCK * d: BLOCK * (d + 1)] for d in range(3)]
        if t == 0:
            s[0] = s[0] + jnp.where(j == 0, NEG, 0.0)
        if t == nq - 1:
            s[2] = s[2] + jnp.where(j == nsteps - 1, NEG, 0.0)
        m = jnp.maximum(jnp.max(jnp.maximum(jnp.maximum(s[0], s[1]), s[2]), axis=1, keepdims=True), sink)
        p = [jnp.exp2(sd - m) for sd in s]
        l = jnp.sum(p[0] + p[1] + p[2], axis=1, keepdims=True) + jnp.exp2(sink - m)
        o = sum(jnp.dot(p[d].astype(BF16), kv_block(vp_ref, vc_ref, vn_ref, t - 1 + d), preferred_element_type=F32)
                for d in range(3)) / l
        for g in range(G):
            gate = g_ref[0, rows, LANES * g: LANES * (g + 1)]
            o_ref[0, rows, LANES * g: LANES * (g + 1)] = (
                o[BLOCK * g: BLOCK * (g + 1)] * (gate * jax.nn.sigmoid(gate))).astype(BF16)


def _window_attn(qkv, q_blk0, k_blk0, v_blk0, gate, g_blk0, bias, sink, *, B, S):
    G = WIN_HEADS // WIN_KV
    nb = S // BLOCK
    nq = 16 if nb % 16 == 0 else (4 if nb % 4 == 0 else 1)
    nsteps = nb // nq
    kern = functools.partial(_win_kernel, G=G, nq=nq, nsteps=nsteps)

    def kv_specs(blk0):
        return [pl.BlockSpec((1, BLOCK, LANES), lambda b, h, j: (b, jnp.maximum(j * nq - 1, 0), blk0 + h)),
                pl.BlockSpec((1, nq * BLOCK, LANES), lambda b, h, j: (b, j, blk0 + h)),
                pl.BlockSpec((1, BLOCK, LANES), lambda b, h, j: (b, jnp.minimum((j + 1) * nq, nb - 1), blk0 + h))]

    return pl.pallas_call(
        kern,
        grid=(B, WIN_KV, nsteps),
        in_specs=[pl.BlockSpec((1, nq * BLOCK, G * LANES), lambda b, h, j: (b, j, q_blk0 + h))]
        + kv_specs(k_blk0) + kv_specs(v_blk0)
        + [pl.BlockSpec((1, G * BLOCK, 3 * BLOCK), lambda b, h, j: (h, 0, 0)),
           pl.BlockSpec((1, G * BLOCK, 1), lambda b, h, j: (h, 0, 0)),
           pl.BlockSpec((1, nq * BLOCK, G * LANES), lambda b, h, j: (b, j, g_blk0 + h))],
        out_specs=pl.BlockSpec((1, nq * BLOCK, G * LANES), lambda b, h, j: (b, j, h)),
        out_shape=jax.ShapeDtypeStruct((B, S, WIN_HEADS * LANES), BF16),
        compiler_params=_cparams(("parallel", "parallel", "arbitrary")),
        name="window_attn",
    )(qkv, qkv, qkv, qkv, qkv, qkv, qkv, bias, sink, gate)


def _t5_bucket(rel):
    half = N_BUCKETS // 2
    max_exact = half // 2
    ret = jnp.where(rel > 0, half, 0)
    n = jnp.abs(rel)
    nf = jnp.maximum(n, 1).astype(F32)
    large = max_exact + (jnp.log(nf / max_exact) / math.log(MAX_DIST / max_exact)
                         * (half - max_exact)).astype(jnp.int32)
    large = jnp.minimum(large, half - 1)
    return ret + jnp.where(n < max_exact, n, large)


def _bias_kernel(bucket_ref, rb_ref, o_ref):
    bucket = bucket_ref[...]
    row = lax.broadcasted_iota(jnp.int32, bucket.shape, 0)
    colk = lax.broadcasted_iota(jnp.int32, bucket.shape, 1)
    inband = jnp.abs(colk - WINDOW - row) <= WINDOW
    for h in range(WIN_HEADS):
        acc = jnp.zeros(bucket.shape, F32)
        for b in range(N_BUCKETS):
            acc = jnp.where(bucket == b, rb_ref[b, h], acc)
        o_ref[h] = jnp.where(inband, acc * LOG2E, NEG)


def _window_bias_table(rel_bias):
    span = BLOCK + 2 * WINDOW
    rel = jnp.arange(span)[None, :] - WINDOW - jnp.arange(BLOCK)[:, None]
    bias = pl.pallas_call(
        _bias_kernel,
        in_specs=[pl.BlockSpec(memory_space=pltpu.VMEM), pl.BlockSpec(memory_space=pltpu.SMEM)],
        out_specs=pl.BlockSpec(memory_space=pltpu.VMEM),
        out_shape=jax.ShapeDtypeStruct((WIN_HEADS, BLOCK, span), F32),
        name="t5_bias_table",
    )(_t5_bucket(rel).astype(jnp.int32), rel_bias.astype(F32))
    G = WIN_HEADS // WIN_KV
    return bias.reshape(WIN_KV, G * BLOCK, span)


def _branch_kernel(*refs, tn):
    a, w, m, o_ref = refs[:3], refs[3:6], refs[6:-1], refs[-1]
    nj = o_ref.shape[1] // tn
    for j in range(nj):
        cols = slice(tn * j, tn * (j + 1))
        y = None
        for k in range(3):
            t = jax.nn.sigmoid(m[nj * k + j][...]) * jnp.dot(a[k][...], w[k][:, cols], preferred_element_type=F32)
            y = t if y is None else y + t
        o_ref[:, cols] = y.astype(BF16)


def _branch_merge(a_list, w_list, gate, m_col0):
    T, W = a_list[0].shape
    D = w_list[0].shape[1]
    tm = _pick(T, 512)
    tn = _pick(D, 1024)
    nj = D // tn
    a_spec = pl.BlockSpec((tm, W), lambda i: (i, 0))
    w_spec = pl.BlockSpec((W, D), lambda i: (0, 0), pipeline_mode=pl.Buffered(1))

    def m_spec(k, j):
        blk = (m_col0 + k * D) // tn + j
        return pl.BlockSpec((tm, tn), lambda i: (i, blk))

    assert m_col0 % tn == 0
    m_specs = [m_spec(k, j) for k in range(3) for j in range(nj)]
    return pl.pallas_call(
        functools.partial(_branch_kernel, tn=tn),
        grid=(T // tm,),
        in_specs=[a_spec] * 3 + [w_spec] * 3 + m_specs,
        out_specs=pl.BlockSpec((tm, D), lambda i: (i, 0)),
        out_shape=jax.ShapeDtypeStruct((T, D), BF16),
        compiler_params=_cparams(("parallel",)),
        name="branch_merge",
    )(*a_list, *w_list, *([gate] * (3 * nj)))


def _out_kernel(y_ref, w_ref, x_ref, gt_ref, *rest, rc, with_next):
    if with_next:
        g_ref, sh_ref, sc_ref, o_ref, hn_ref = rest
    else:
        (o_ref,) = rest
    for r in range(y_ref.shape[1] // rc):
        rows = slice(rc * r, rc * (r + 1))
        x = x_ref[0, rows, :] + gt_ref[0] * jnp.dot(y_ref[0, rows, :], w_ref[...], preferred_element_type=F32)
        o_ref[0, rows, :] = x
        if with_next:
            rms = lax.rsqrt(jnp.mean(x * x, axis=-1, keepdims=True) + EPS)
            hn_ref[0, rows, :] = ((x * rms * g_ref[...]) * (1.0 + sc_ref[0]) + sh_ref[0]).astype(BF16)


def _out_proj(y, w_out, x, mod, ln_g_next=None, mod_next=None):
    B, S, D = x.shape
    tm = _pick(S, 512)
    with_next = ln_g_next is not None
    row_spec = pl.BlockSpec((1, tm, D), lambda b, i: (b, i, 0))
    in_specs = [row_spec,
                pl.BlockSpec((D, D), lambda b, i: (0, 0)),
                row_spec,
                pl.BlockSpec((1, 1, D), lambda b, i: (b, 0, 2))]
    args = [y, w_out, x, mod]
    out_specs = [row_spec]
    out_shape = [jax.ShapeDtypeStruct((B, S, D), F32)]
    if with_next:
        in_specs += [pl.BlockSpec((1, D), lambda b, i: (0, 0)),
                     pl.BlockSpec((1, 1, D), lambda b, i: (b, 0, 0)),
                     pl.BlockSpec((1, 1, D), lambda b, i: (b, 0, 1))]
        args += [ln_g_next.reshape(1, D), mod_next, mod_next]
        out_specs.append(row_spec)
        out_shape.append(jax.ShapeDtypeStruct((B, S, D), BF16))
    res = pl.pallas_call(
        functools.partial(_out_kernel, rc=min(256, tm), with_next=with_next),
        grid=(B, S // tm),
        in_specs=in_specs,
        out_specs=out_specs,
        out_shape=out_shape,
        compiler_params=_cparams(("parallel", "parallel")),
        name="out_proj",
    )(*args)
    return (res[0], res[1]) if with_next else (res[0], None)


def _split_points(D):
    sizes = [Q_LORA, KV_LORA, MLA_ROPE, MLA_HEADS * MLA_V,
             AX_HEADS * AX_HD, AX_KV * AX_HD, AX_KV * AX_HD, AX_HEADS * AX_HD,
             WIN_HEADS * WIN_HD, WIN_KV * WIN_HD, WIN_KV * WIN_HD, WIN_HEADS * WIN_HD,
             D, D, D]
    pts, acc = [0], 0
    for s in sizes:
        acc += s
        pts.append(acc)
    return pts


def _prep_weights(D, w_in, w_q_up, w_kv_up, q_a_norm, kv_a_norm, mla_qn, mla_kn,
                  ax_qn, ax_kn, win_qn, win_kn, w_br_mla, w_br_ax, w_br_win, w_out):
    p = _split_points(D)
    col = lambda k: w_in[:, :, p[k]:p[k + 1]]
    (cq, ckv, kr, g_mla, aq, ak, av, g_ax, wq, wk, wv, g_win, m_mla, m_ax, m_win) = [col(k) for k in range(15)]
    L = w_in.shape[0]
    w = {}
    qkv_cols = [_pair_layout(aq), _pair_layout(ak), av, wq, wk, wv, cq, ckv,
                _pair_layout(jnp.concatenate([kr, kr], axis=-1))]
    gate_cols = [g_mla, g_ax, g_win, m_mla, m_ax, m_win]
    n_qkv = sum(c.shape[-1] for c in qkv_cols)
    pad = jnp.zeros(w_in.shape[:2] + (GATE_COL0 - n_qkv,), w_in.dtype)
    w["in"] = jnp.concatenate(qkv_cols + [pad] + gate_cols, axis=-1).astype(BF16)
    w["n_qkv"] = n_qkv
    w["n_gate"] = sum(c.shape[-1] for c in gate_cols)
    wq4 = w_q_up.reshape(L, Q_LORA, MLA_HEADS, MLA_QK)
    w["q_up"] = jnp.concatenate([wq4[..., :MLA_NOPE].reshape(L, Q_LORA, -1),
                                 _pair_layout(wq4[..., MLA_NOPE:].reshape(L, Q_LORA, -1))], axis=-1).astype(BF16)
    wkv4 = w_kv_up.reshape(L, KV_LORA, MLA_HEADS, MLA_NOPE + MLA_V)
    w["kv_up"] = jnp.concatenate([wkv4[..., :MLA_NOPE].reshape(L, KV_LORA, -1),
                                  wkv4[..., MLA_NOPE:].reshape(L, KV_LORA, -1)], axis=-1).astype(BF16)
    qs_mla = MLA_QK ** -0.5 * LOG2E
    pair_gain = lambda g: jnp.concatenate(
        [g[:, :MLA_NOPE], _pair_layout(jnp.concatenate([g[:, MLA_NOPE:], g[:, MLA_NOPE:]], axis=-1))], axis=-1)
    w["gq"] = (pair_gain(mla_qn) * qs_mla).reshape(L, 1, 2 * LANES)
    w["gk"] = pair_gain(mla_kn).reshape(L, 1, 2 * LANES)
    ones = jnp.ones((L, AX_KV * AX_HD), F32)
    w["qkv_gain"] = jnp.concatenate(
        [jnp.tile(_pair_layout(ax_qn), (1, AX_HEADS)) * (AX_HD ** -0.5 * LOG2E),
         jnp.tile(_pair_layout(ax_kn), (1, AX_KV)), ones,
         jnp.tile(win_qn, (1, WIN_HEADS)) * (WIN_HD ** -0.5 * LOG2E), jnp.tile(win_kn, (1, WIN_KV)), ones,
         q_a_norm, kv_a_norm], axis=-1).reshape(L, 1, -1)
    w["br"] = [w_br_mla.astype(BF16), w_br_ax.astype(BF16), w_br_win.astype(BF16)]
    w["out"] = w_out.astype(BF16)
    return w


def _layer(x, h, mod, mod_next, l, ln_g, w, tabs, bias_tab, sink_tab):
    B, S, D = x.shape
    T = B * S
    h2 = h.reshape(T, D)
    cos_m, sin_m, cos_a, sin_a = tabs[S]
    qkv, kr2 = _qkv_proj(h2, w["in"][l], w["n_qkv"], w["qkv_gain"][l], cos_a, sin_a, S)
    q_m, k_m, v_m = _mla_prep(qkv, 2 * LANES * len(QKV_KINDS), kr2, w["q_up"][l], w["kv_up"][l],
                              w["gq"][l], w["gk"][l], cos_m, sin_m, S)
    gate = _gate_proj(h2, w["in"][l], w["n_gate"])
    r3 = lambda a: a.reshape(B, S, a.shape[-1])
    q_m, k_m, v_m, qkv3, gate3 = r3(q_m), r3(k_m), r3(v_m), r3(qkv), r3(gate)
    a_mla = _flash_attn(q_m, 0, k_m, 0, v_m, 0, gate3, 0,
                        B=B, S=S, H=MLA_HEADS, KV=MLA_HEADS, dk=2 * LANES, rows=min(1024, S))
    gq = AX_HEADS // AX_KV
    a_ax = _flash_attn(qkv3, 0, qkv3, AX_HEADS, qkv3, AX_HEADS + AX_KV, gate3, AX_KV,
                       B=B, S=S, H=AX_HEADS, KV=AX_KV, dk=AX_HD, rows=min(1024, S))
    wbase = AX_HEADS + 2 * AX_KV
    a_win = _window_attn(qkv3, wbase // gq, wbase + WIN_HEADS, wbase + WIN_HEADS + WIN_KV,
                         gate3, 2 * AX_KV, bias_tab, sink_tab[l], B=B, S=S)
    a_list = [a.reshape(T, a.shape[-1]) for a in (a_mla, a_ax, a_win)]
    y = _branch_merge(a_list, [wb[l] for wb in w["br"]], gate, 3 * MLA_HEADS * MLA_V)
    if mod_next is None:
        return _out_proj(y.reshape(B, S, D), w["out"][l], x, mod)
    return _out_proj(y.reshape(B, S, D), w["out"][l], x, mod, ln_g[l + 1], mod_next)


def kernel(x_prompt, x_sample, c_prompt, c_sample, ln_g, w_ada, b_ada, w_in, q_a_norm, w_q_up, kv_a_norm, w_kv_up, mla_qn, mla_kn, ax_qn, ax_kn, win_qn, win_kn, win_sink, rel_bias, w_br_mla, w_br_ax, w_br_win, w_out):
    L = ln_g.shape[0]
    D = x_prompt.shape[-1]
    Bp = x_prompt.shape[0]
    w = _prep_weights(D, w_in, w_q_up, w_kv_up, q_a_norm, kv_a_norm, mla_qn, mla_kn,
                      ax_qn, ax_kn, win_qn, win_kn, w_br_mla, w_br_ax, w_br_win, w_out)
    mod_all = _modulation(jnp.concatenate([c_prompt, c_sample], axis=0), w_ada, b_ada)
    tabs = {}
    for S in {x_prompt.shape[1], x_sample.shape[1]}:
        pos = jnp.arange(S, dtype=F32)
        row = jnp.repeat(jnp.arange(S // GRID_W, dtype=F32), GRID_W)
        colp = jnp.tile(jnp.arange(GRID_W, dtype=F32), S // GRID_W)
        tabs[S] = _rope_tables(pos, pos) + _rope_tables(row, colp)
    bias_tab = _window_bias_table(rel_bias)
    G = WIN_HEADS // WIN_KV
    sink_tab = jnp.repeat(win_sink.astype(F32) * LOG2E, BLOCK, axis=-1).reshape(L, WIN_KV, G * BLOCK, 1)
    mods_p = [mod_all[l, :Bp][:, None, :] for l in range(L)]
    mods_s = [mod_all[l, Bp:][:, None, :] for l in range(L)]
    yp, ys = x_prompt, x_sample
    hp = _norm_modulate(yp, ln_g[0], mods_p[0])
    hs = _norm_modulate(ys, ln_g[0], mods_s[0])
    for l in range(L):
        nxt = l + 1 < L
        yp, hp = _layer(yp, hp, mods_p[l], mods_p[l + 1] if nxt else None, l, ln_g, w, tabs, bias_tab, sink_tab)
        ys, hs = _layer(ys, hs, mods_s[l], mods_s[l + 1] if nxt else None, l, ln_g, w, tabs, bias_tab, sink_tab)
    return (yp, ys)
```

```python
import functools
import math

import jax
import jax.numpy as jnp
from jax import lax
from jax.experimental import pallas as pl
from jax.experimental.pallas import tpu as pltpu

F32 = jnp.float32
BF16 = jnp.bfloat16

GRID_W = 64
BLOCK = 128
EPS = 1e-6
ROPE_THETA = 10000.0
NEG = -1e30
LOG2E = 1.4426950408889634
LANES = 128

MLA_HEADS = 8
MLA_NOPE = 128
MLA_ROPE = 64
MLA_QK = MLA_NOPE + MLA_ROPE
MLA_V = 128
Q_LORA = 512
KV_LORA = 256
AX_HEADS = 8
AX_KV = 2
AX_HD = 128
WIN_HEADS = 8
WIN_KV = 2
WIN_HD = 128
WINDOW = 128
N_BUCKETS = 32
MAX_DIST = 128
ROPE_HALF = 32

VMEM_LIMIT = 56 * 1024 * 1024
GATE_COL0 = 4096

MOD_TN = 512
NORM_TS = 512
QKV_TM, QKV_RC = 512, 256
MLA_PREP_TM, MLA_PREP_RC = 1024, 128
GATE_TM, GATE_TN = 2048, 512
FLASH_ROWS, FLASH_TK, FLASH_NT = 1024, 2048, 4
WIN_NQ = (16, 4, 1)
BR_TM, BR_TN = 512, 1024
OUT_TM, OUT_RC = 512, 256


def _cparams(sem):
    return pltpu.CompilerParams(dimension_semantics=sem, vmem_limit_bytes=VMEM_LIMIT)


def _pick(n, pref):
    t = min(pref, n)
    while n % t:
        t //= 2
    return t


def _mod_kernel(c_ref, w_ref, b_ref, o_ref):
    o_ref[0] = jnp.dot(c_ref[...], w_ref[0].astype(BF16), preferred_element_type=F32) + b_ref[0]


def _modulation(c_all, w_ada, b_ada):
    L, D, N = w_ada.shape
    R = c_all.shape[0]
    tn = _pick(N, MOD_TN)
    return pl.pallas_call(
        _mod_kernel,
        grid=(L, N // tn),
        in_specs=[pl.BlockSpec((R, D), lambda l, j: (0, 0)),
                  pl.BlockSpec((1, D, tn), lambda l, j: (l, 0, j)),
                  pl.BlockSpec((1, 1, tn), lambda l, j: (l, 0, j))],
        out_specs=pl.BlockSpec((1, R, tn), lambda l, j: (l, 0, j)),
        out_shape=jax.ShapeDtypeStruct((L, R, N), F32),
        compiler_params=_cparams(("parallel", "parallel")),
        name="adaln_mod",
    )(c_all.astype(BF16), w_ada, b_ada.reshape(L, 1, N))


def _h_kernel(x_ref, g_ref, sh_ref, sc_ref, o_ref):
    x = x_ref[0]
    r = lax.rsqrt(jnp.mean(x * x, axis=-1, keepdims=True) + EPS)
    y = x * r * g_ref[...]
    o_ref[0] = (y * (1.0 + sc_ref[0]) + sh_ref[0]).astype(BF16)


def _norm_modulate(x, ln_g, mod):
    B, S, D = x.shape
    ts = _pick(S, NORM_TS)
    return pl.pallas_call(
        _h_kernel,
        grid=(B, S // ts),
        in_specs=[pl.BlockSpec((1, ts, D), lambda b, i: (b, i, 0)),
                  pl.BlockSpec((1, D), lambda b, i: (0, 0)),
                  pl.BlockSpec((1, 1, D), lambda b, i: (b, 0, 0)),
                  pl.BlockSpec((1, 1, D), lambda b, i: (b, 0, 1))],
        out_specs=pl.BlockSpec((1, ts, D), lambda b, i: (b, i, 0)),
        out_shape=jax.ShapeDtypeStruct((B, S, D), BF16),
        compiler_params=_cparams(("parallel", "parallel")),
        name="norm_modulate",
    )(x, ln_g.reshape(1, D), mod, mod)


def _pair_layout(x):
    shp = x.shape
    x = x.reshape(shp[:-1] + (shp[-1] // LANES, 2, 2, ROPE_HALF))
    return jnp.swapaxes(x, -3, -2).reshape(shp)


def _rope128(y, cos, sin_signed):
    return y * cos + pltpu.roll(y, LANES // 2, 1) * sin_signed


def _rope_tables(pos_a, pos_b):
    freqs = ROPE_THETA ** (-jnp.arange(ROPE_HALF, dtype=F32) / ROPE_HALF)
    ang_a = pos_a[:, None] * freqs[None, :]
    ang_b = pos_b[:, None] * freqs[None, :]
    cos = jnp.concatenate([jnp.cos(ang_a), jnp.cos(ang_b)], axis=-1)
    sin = jnp.concatenate([jnp.sin(ang_a), jnp.sin(ang_b)], axis=-1)
    return jnp.concatenate([cos, cos], axis=-1), jnp.concatenate([-sin, sin], axis=-1)


def _mla_prep_kernel(cqn_ref, ckvn_ref, kr_ref, wq_ref, wkv_ref, gq_ref, gk_ref,
                     cos_ref, sin_ref, q_ref, k_ref, v_ref, *, rc):
    for r in range(cqn_ref.shape[0] // rc):
        _mla_prep_rows(slice(rc * r, rc * (r + 1)), cqn_ref, ckvn_ref, kr_ref, wq_ref, wkv_ref,
                       gq_ref, gk_ref, cos_ref, sin_ref, q_ref, k_ref, v_ref)


def _mla_prep_rows(rows, cqn_ref, ckvn_ref, kr_ref, wq_ref, wkv_ref, gq_ref, gk_ref,
                   cos_ref, sin_ref, q_ref, k_ref, v_ref):
    kr2 = kr_ref[rows, :]
    qraw = jnp.dot(cqn_ref[rows, :], wq_ref[...], preferred_element_type=F32)
    kvraw = jnp.dot(ckvn_ref[rows, :], wkv_ref[...], preferred_element_type=F32)
    cos = cos_ref[rows, :]
    sin = sin_ref[rows, :]
    lane = lax.broadcasted_iota(jnp.int32, cos.shape, 1)
    even = (lane % (2 * ROPE_HALF)) < ROPE_HALF
    gq = gq_ref[...]
    gk = gk_ref[...]
    nope_w = MLA_HEADS * MLA_NOPE
    inv_qk = 1.0 / MLA_QK
    kr_sq = kr2 * kr2
    k_rope = _rope128(kr2 * gk[:, LANES:], cos, sin)
    for p in range(MLA_HEADS // 2):
        qr = qraw[:, nope_w + LANES * p: nope_w + LANES * (p + 1)]
        qr_sq = qr * qr
        q_rope = _rope128(qr * gq[:, LANES:], cos, sin)
        for e in range(2):
            hd = 2 * p + e
            keep = even if e == 0 else jnp.logical_not(even)
            qn = qraw[:, LANES * hd: LANES * (hd + 1)]
            rq = lax.rsqrt(jnp.sum(qn * qn + jnp.where(keep, qr_sq, 0.0), axis=-1, keepdims=True) * inv_qk + EPS)
            q_ref[rows, 2 * LANES * hd: 2 * LANES * hd + LANES] = (qn * rq * gq[:, :LANES]).astype(BF16)
            q_ref[rows, 2 * LANES * hd + LANES: 2 * LANES * (hd + 1)] = jnp.where(keep, q_rope * rq, 0.0).astype(BF16)
            kn = kvraw[:, LANES * hd: LANES * (hd + 1)]
            rk = lax.rsqrt(jnp.sum(kn * kn + jnp.where(keep, kr_sq, 0.0), axis=-1, keepdims=True) * inv_qk + EPS)
            k_ref[rows, 2 * LANES * hd: 2 * LANES * hd + LANES] = (kn * rk * gk[:, :LANES]).astype(BF16)
            k_ref[rows, 2 * LANES * hd + LANES: 2 * LANES * (hd + 1)] = jnp.where(keep, k_rope * rk, 0.0).astype(BF16)
    v_ref[rows, :] = kvraw[:, nope_w:].astype(BF16)


def _mla_prep(lat, lat_col0, kr2, wq, wkv, gq, gk, cos, sin, S):
    T = lat.shape[0]
    tm = _pick(S, MLA_PREP_TM)
    nS = S // tm
    hq = MLA_HEADS * 2 * LANES
    const = lambda i: (0, 0)
    return pl.pallas_call(
        functools.partial(_mla_prep_kernel, rc=min(MLA_PREP_RC, tm)),
        grid=(T // tm,),
        in_specs=[pl.BlockSpec((tm, Q_LORA), lambda i: (i, lat_col0 // Q_LORA)),
                  pl.BlockSpec((tm, KV_LORA), lambda i: (i, (lat_col0 + Q_LORA) // KV_LORA)),
                  pl.BlockSpec((tm, LANES), lambda i: (i, 0)),
                  pl.BlockSpec(wq.shape, const),
                  pl.BlockSpec(wkv.shape, const),
                  pl.BlockSpec((1, 2 * LANES), const),
                  pl.BlockSpec((1, 2 * LANES), const),
                  pl.BlockSpec((tm, LANES), lambda i: (i % nS, 0)),
                  pl.BlockSpec((tm, LANES), lambda i: (i % nS, 0))],
        out_specs=[pl.BlockSpec((tm, hq), lambda i: (i, 0)),
                   pl.BlockSpec((tm, hq), lambda i: (i, 0)),
                   pl.BlockSpec((tm, MLA_HEADS * MLA_V), lambda i: (i, 0))],
        out_shape=[jax.ShapeDtypeStruct((T, hq), BF16),
                   jax.ShapeDtypeStruct((T, hq), BF16),
                   jax.ShapeDtypeStruct((T, MLA_HEADS * MLA_V), BF16)],
        compiler_params=_cparams(("parallel",)),
        name="mla_prep",
    )(lat, lat, kr2, wq, wkv, gq, gk, cos, sin)


QKV_KINDS = (0, 0, 0, 0, 0, 2, 1, 1, 1, 1, 1, 2)


def _qkv_kernel(h_ref, w_ref, gain_ref, cos_ref, sin_ref, o_ref, kr_ref, *, rc):
    tm = h_ref.shape[0]
    tn = 2 * LANES
    c0 = tn * len(QKV_KINDS)
    for r in range(tm // rc):
        rows = slice(rc * r, rc * (r + 1))
        for lo, width in ((c0, Q_LORA), (c0 + Q_LORA, KV_LORA)):
            x = jnp.dot(h_ref[rows, :], w_ref[:, lo: lo + width], preferred_element_type=F32)
            y = x * lax.rsqrt(jnp.mean(x * x, axis=-1, keepdims=True) + EPS) * gain_ref[:, lo: lo + width]
            o_ref[rows, lo: lo + width] = y.astype(BF16)
        kr_ref[rows, :] = jnp.dot(h_ref[rows, :], w_ref[:, c0 + Q_LORA + KV_LORA:], preferred_element_type=F32)
        for t, kind in enumerate(QKV_KINDS):
            acc = jnp.dot(h_ref[rows, :], w_ref[:, tn * t: tn * (t + 1)], preferred_element_type=F32)
            if kind == 2:
                o_ref[rows, tn * t: tn * (t + 1)] = acc.astype(BF16)
                continue
            for e in range(2):
                lanes = slice(tn * t + LANES * e, tn * t + LANES * (e + 1))
                x = acc[:, LANES * e: LANES * (e + 1)]
                y = x * lax.rsqrt(jnp.mean(x * x, axis=-1, keepdims=True) + EPS) * gain_ref[:, lanes]
                if kind == 0:
                    y = _rope128(y, cos_ref[rows, :], sin_ref[rows, :])
                o_ref[rows, lanes] = y.astype(BF16)


def _qkv_proj(h2, w, N, gains, cos, sin, S):
    T, D = h2.shape
    No = N - LANES
    tm = _pick(S, QKV_TM)
    nS = S // tm
    return pl.pallas_call(
        functools.partial(_qkv_kernel, rc=min(QKV_RC, tm)),
        grid=(T // tm,),
        in_specs=[pl.BlockSpec((tm, D), lambda i: (i, 0)),
                  pl.BlockSpec((D, N), lambda i: (0, 0), pipeline_mode=pl.Buffered(1)),
                  pl.BlockSpec((1, No), lambda i: (0, 0)),
                  pl.BlockSpec((tm, LANES), lambda i: (i % nS, 0)),
                  pl.BlockSpec((tm, LANES), lambda i: (i % nS, 0))],
        out_specs=[pl.BlockSpec((tm, No), lambda i: (i, 0)),
                   pl.BlockSpec((tm, LANES), lambda i: (i, 0))],
        out_shape=[jax.ShapeDtypeStruct((T, No), BF16),
                   jax.ShapeDtypeStruct((T, LANES), F32)],
        compiler_params=_cparams(("parallel",)),
        name="qkv_proj",
    )(h2, w, gains, cos, sin)


def _gate_kernel(h_ref, w_ref, o_ref):
    o_ref[...] = jnp.dot(h_ref[...], w_ref[...], preferred_element_type=F32)


def _gate_proj(h2, w, N):
    T, D = h2.shape
    tm = _pick(T, GATE_TM)
    tn = _pick(N, GATE_TN)
    j0 = GATE_COL0 // tn
    return pl.pallas_call(
        _gate_kernel,
        grid=(T // tm, N // tn),
        in_specs=[pl.BlockSpec((tm, D), lambda i, j: (i, 0)),
                  pl.BlockSpec((D, tn), lambda i, j: (0, j0 + j))],
        out_specs=pl.BlockSpec((tm, tn), lambda i, j: (i, j)),
        out_shape=jax.ShapeDtypeStruct((T, N), F32),
        compiler_params=_cparams(("parallel", "arbitrary")),
        name="gate_proj",
    )(h2, w)


def _flash_kernel(q_ref, k_ref, v_ref, g_ref, o_ref, vext_ref, *, G, dk, tq, tk, S, nt):
    qi = pl.program_id(2)

    def build_vext():
        lane = lax.broadcasted_iota(jnp.int32, (S, LANES), 1)
        vext_ref[:, :LANES] = v_ref[0]
        vext_ref[:, LANES:] = jnp.where(lane == 0, 1.0, 0.0).astype(BF16)

    if nt * tq == S:
        build_vext()
    else:
        pl.when(qi == 0)(build_vext)
    n = S // tk
    for t in range(nt):
        qrows = slice(tq * t, tq * (t + 1))
        if G == 1:
            q = q_ref[0, qrows, :]
        else:
            q = jnp.concatenate([q_ref[0, qrows, dk * g: dk * (g + 1)] for g in range(G)], axis=0)

        def scores(c, q=q):
            k = k_ref[0, tk * c: tk * (c + 1), :]
            return lax.dot_general(q, k, (((1,), (1,)), ((), ())), preferred_element_type=F32)

        s_next = scores(0)
        m = None
        acc = None
        for c in range(n):
            s = s_next
            if c + 1 < n:
                s_next = scores(c + 1)
            m_blk = jnp.max(s, axis=1, keepdims=True)
            m_new = m_blk if m is None else jnp.maximum(m, m_blk)
            p = jnp.exp2(s - m_new).astype(BF16)
            pv = jnp.dot(p, vext_ref[tk * c: tk * (c + 1), :], preferred_element_type=F32)
            acc = pv if acc is None else jnp.exp2(m - m_new) * acc + pv
            m = m_new
        o = acc[:, :LANES] / acc[:, LANES:LANES + 1]
        for g in range(G):
            gate = g_ref[0, qrows, LANES * g: LANES * (g + 1)]
            o_ref[0, qrows, LANES * g: LANES * (g + 1)] = (
                o[tq * g: tq * (g + 1)] * (gate * jax.nn.sigmoid(gate))).astype(BF16)


def _flash_attn(q_arr, q_blk0, k_arr, k_blk0, v_arr, v_blk0, g_arr, g_blk0, *, B, S, H, KV, dk, rows):
    G = H // KV
    tq = rows // G
    tk = _pick(S // 2, FLASH_TK)
    nt = min(FLASH_NT, S // tq)
    kern = functools.partial(_flash_kernel, G=G, dk=dk, tq=tq, tk=tk, S=S, nt=nt)
    return pl.pallas_call(
        kern,
        grid=(B, KV, S // (tq * nt)),
        in_specs=[pl.BlockSpec((1, tq * nt, G * dk), lambda b, h, i: (b, i, q_blk0 + h)),
                  pl.BlockSpec((1, S, dk), lambda b, h, i: (b, 0, k_blk0 + h)),
                  pl.BlockSpec((1, S, LANES), lambda b, h, i: (b, 0, v_blk0 + h)),
                  pl.BlockSpec((1, tq * nt, G * LANES), lambda b, h, i: (b, i, g_blk0 + h))],
        out_specs=pl.BlockSpec((1, tq * nt, G * LANES), lambda b, h, i: (b, i, h)),
        out_shape=jax.ShapeDtypeStruct((B, S, H * LANES), BF16),
        scratch_shapes=[pltpu.VMEM((S, 2 * LANES), BF16)],
        compiler_params=_cparams(("parallel", "parallel", "arbitrary")),
        name=f"flash_attn_g{G}",
    )(q_arr, k_arr, v_arr, g_arr)


def _win_kernel(q_ref, kp_ref, kc_ref, kn_ref, vp_ref, vc_ref, vn_ref, bias_ref, sink_ref, g_ref, o_ref, *, G, nq, nsteps):
    j = pl.program_id(2)
    nt = (((1,), (1,)), ((), ()))
    sink = sink_ref[0]

    def kv_block(prev_ref, cur_ref, next_ref, t):
        if t < 0:
            return prev_ref[0]
        if t >= nq:
            return next_ref[0]
        return cur_ref[0, BLOCK * t: BLOCK * (t + 1), :]

    for t in range(nq):
        rows = slice(BLOCK * t, BLOCK * (t + 1))
        q = jnp.concatenate([q_ref[0, rows, LANES * g: LANES * (g + 1)] for g in range(G)], axis=0)
        s = [lax.dot_general(q, kv_block(kp_ref, kc_ref, kn_ref, t - 1 + d), nt, preferred_element_type=F32)
             + bias_ref[0, :, BLOCK * d: BLOCK * (d + 1)] for d in range(3)]
        if t == 0:
            s[0] = s[0] + jnp.where(j == 0, NEG, 0.0)
        if t == nq - 1:
            s[2] = s[2] + jnp.where(j == nsteps - 1, NEG, 0.0)
        m = jnp.maximum(jnp.max(jnp.maximum(jnp.maximum(s[0], s[1]), s[2]), axis=1, keepdims=True), sink)
        p = [jnp.exp2(sd - m) for sd in s]
        l = jnp.sum(p[0] + p[1] + p[2], axis=1, keepdims=True) + jnp.exp2(sink - m)
        o = sum(jnp.dot(p[d].astype(BF16), kv_block(vp_ref, vc_ref, vn_ref, t - 1 + d), preferred_element_type=F32)
                for d in range(3)) / l
        for g in range(G):
            gate = g_ref[0, rows, LANES * g: LANES * (g + 1)]
            o_ref[0, rows, LANES * g: LANES * (g + 1)] = (
                o[BLOCK * g: BLOCK * (g + 1)] * (gate * jax.nn.sigmoid(gate))).astype(BF16)


def _window_attn(qkv, q_blk0, k_blk0, v_blk0, gate, g_blk0, bias, sink, *, B, S):
    G = WIN_HEADS // WIN_KV
    nb = S // BLOCK
    nq = next(n for n in WIN_NQ if nb % n == 0)
    nsteps = nb // nq
    kern = functools.partial(_win_kernel, G=G, nq=nq, nsteps=nsteps)

    def kv_specs(blk0):
        return [pl.BlockSpec((1, BLOCK, LANES), lambda b, h, j: (b, jnp.maximum(j * nq - 1, 0), blk0 + h)),
                pl.BlockSpec((1, nq * BLOCK, LANES), lambda b, h, j: (b, j, blk0 + h)),
                pl.BlockSpec((1, BLOCK, LANES), lambda b, h, j: (b, jnp.minimum((j + 1) * nq, nb - 1), blk0 + h))]

    return pl.pallas_call(
        kern,
        grid=(B, WIN_KV, nsteps),
        in_specs=[pl.BlockSpec((1, nq * BLOCK, G * LANES), lambda b, h, j: (b, j, q_blk0 + h))]
        + kv_specs(k_blk0) + kv_specs(v_blk0)
        + [pl.BlockSpec((1, G * BLOCK, 3 * BLOCK), lambda b, h, j: (h, 0, 0)),
           pl.BlockSpec((1, G * BLOCK, 1), lambda b, h, j: (h, 0, 0)),
           pl.BlockSpec((1, nq * BLOCK, G * LANES), lambda b, h, j: (b, j, g_blk0 + h))],
        out_specs=pl.BlockSpec((1, nq * BLOCK, G * LANES), lambda b, h, j: (b, j, h)),
        out_shape=jax.ShapeDtypeStruct((B, S, WIN_HEADS * LANES), BF16),
        compiler_params=_cparams(("parallel", "parallel", "arbitrary")),
        name="window_attn",
    )(qkv, qkv, qkv, qkv, qkv, qkv, qkv, bias, sink, gate)


def _t5_bucket(rel):
    half = N_BUCKETS // 2
    max_exact = half // 2
    ret = jnp.where(rel > 0, half, 0)
    n = jnp.abs(rel)
    nf = jnp.maximum(n, 1).astype(F32)
    large = max_exact + (jnp.log(nf / max_exact) / math.log(MAX_DIST / max_exact)
                         * (half - max_exact)).astype(jnp.int32)
    large = jnp.minimum(large, half - 1)
    return ret + jnp.where(n < max_exact, n, large)


def _bias_kernel(bucket_ref, rb_ref, o_ref):
    bucket = bucket_ref[...]
    row = lax.broadcasted_iota(jnp.int32, bucket.shape, 0)
    colk = lax.broadcasted_iota(jnp.int32, bucket.shape, 1)
    inband = jnp.abs(colk - WINDOW - row) <= WINDOW
    for h in range(WIN_HEADS):
        acc = jnp.zeros(bucket.shape, F32)
        for b in range(N_BUCKETS):
            acc = jnp.where(bucket == b, rb_ref[b, h], acc)
        o_ref[h] = jnp.where(inband, acc * LOG2E, NEG)


def _window_bias_table(rel_bias):
    span = BLOCK + 2 * WINDOW
    rel = jnp.arange(span)[None, :] - WINDOW - jnp.arange(BLOCK)[:, None]
    bias = pl.pallas_call(
        _bias_kernel,
        in_specs=[pl.BlockSpec(memory_space=pltpu.VMEM), pl.BlockSpec(memory_space=pltpu.SMEM)],
        out_specs=pl.BlockSpec(memory_space=pltpu.VMEM),
        out_shape=jax.ShapeDtypeStruct((WIN_HEADS, BLOCK, span), F32),
        name="t5_bias_table",
    )(_t5_bucket(rel).astype(jnp.int32), rel_bias.astype(F32))
    G = WIN_HEADS // WIN_KV
    return bias.reshape(WIN_KV, G * BLOCK, span)


def _branch_kernel(*refs, tn):
    a, w, m, o_ref = refs[:3], refs[3:6], refs[6:-1], refs[-1]
    nj = o_ref.shape[1] // tn
    for j in range(nj):
        cols = slice(tn * j, tn * (j + 1))
        y = None
        for k in range(3):
            t = jax.nn.sigmoid(m[nj * k + j][...]) * jnp.dot(a[k][...], w[k][:, cols], preferred_element_type=F32)
            y = t if y is None else y + t
        o_ref[:, cols] = y.astype(BF16)


def _branch_merge(a_list, w_list, gate, m_col0):
    T, W = a_list[0].shape
    D = w_list[0].shape[1]
    tm = _pick(T, BR_TM)
    tn = _pick(D, BR_TN)
    nj = D // tn
    a_spec = pl.BlockSpec((tm, W), lambda i: (i, 0))
    w_spec = pl.BlockSpec((W, D), lambda i: (0, 0), pipeline_mode=pl.Buffered(1))

    def m_spec(k, j):
        blk = (m_col0 + k * D) // tn + j
        return pl.BlockSpec((tm, tn), lambda i: (i, blk))

    assert m_col0 % tn == 0
    m_specs = [m_spec(k, j) for k in range(3) for j in range(nj)]
    return pl.pallas_call(
        functools.partial(_branch_kernel, tn=tn),
        grid=(T // tm,),
        in_specs=[a_spec] * 3 + [w_spec] * 3 + m_specs,
        out_specs=pl.BlockSpec((tm, D), lambda i: (i, 0)),
        out_shape=jax.ShapeDtypeStruct((T, D), BF16),
        compiler_params=_cparams(("parallel",)),
        name="branch_merge",
    )(*a_list, *w_list, *([gate] * (3 * nj)))


def _out_kernel(y_ref, w_ref, x_ref, gt_ref, *rest, rc, with_next):
    if with_next:
        g_ref, sh_ref, sc_ref, o_ref, hn_ref = rest
    else:
        (o_ref,) = rest
    for r in range(y_ref.shape[1] // rc):
        rows = slice(rc * r, rc * (r + 1))
        x = x_ref[0, rows, :] + gt_ref[0] * jnp.dot(y_ref[0, rows, :], w_ref[...], preferred_element_type=F32)
        o_ref[0, rows, :] = x
        if with_next:
            rms = lax.rsqrt(jnp.mean(x * x, axis=-1, keepdims=True) + EPS)
            hn_ref[0, rows, :] = ((x * rms * g_ref[...]) * (1.0 + sc_ref[0]) + sh_ref[0]).astype(BF16)


def _out_proj(y, w_out, x, mod, ln_g_next=None, mod_next=None):
    B, S, D = x.shape
    tm = _pick(S, OUT_TM)
    with_next = ln_g_next is not None
    row_spec = pl.BlockSpec((1, tm, D), lambda b, i: (b, i, 0))
    in_specs = [row_spec,
                pl.BlockSpec((D, D), lambda b, i: (0, 0)),
                row_spec,
                pl.BlockSpec((1, 1, D), lambda b, i: (b, 0, 2))]
    args = [y, w_out, x, mod]
    out_specs = [row_spec]
    out_shape = [jax.ShapeDtypeStruct((B, S, D), F32)]
    if with_next:
        in_specs += [pl.BlockSpec((1, D), lambda b, i: (0, 0)),
                     pl.BlockSpec((1, 1, D), lambda b, i: (b, 0, 0)),
                     pl.BlockSpec((1, 1, D), lambda b, i: (b, 0, 1))]
        args += [ln_g_next.reshape(1, D), mod_next, mod_next]
        out_specs.append(row_spec)
        out_shape.append(jax.ShapeDtypeStruct((B, S, D), BF16))
    res = pl.pallas_call(
        functools.partial(_out_kernel, rc=min(OUT_RC, tm), with_next=with_next),
        grid=(B, S // tm),
        in_specs=in_specs,
        out_specs=out_specs,
        out_shape=out_shape,
        compiler_params=_cparams(("parallel", "parallel")),
        name="out_proj",
    )(*args)
    return (res[0], res[1]) if with_next else (res[0], None)


def _split_points(D):
    sizes = [Q_LORA, KV_LORA, MLA_ROPE, MLA_HEADS * MLA_V,
             AX_HEADS * AX_HD, AX_KV * AX_HD, AX_KV * AX_HD, AX_HEADS * AX_HD,
             WIN_HEADS * WIN_HD, WIN_KV * WIN_HD, WIN_KV * WIN_HD, WIN_HEADS * WIN_HD,
             D, D, D]
    pts, acc = [0], 0
    for s in sizes:
        acc += s
        pts.append(acc)
    return pts


def _prep_weights(D, w_in, w_q_up, w_kv_up, q_a_norm, kv_a_norm, mla_qn, mla_kn,
                  ax_qn, ax_kn, win_qn, win_kn, w_br_mla, w_br_ax, w_br_win, w_out):
    p = _split_points(D)
    col = lambda k: w_in[:, :, p[k]:p[k + 1]]
    (cq, ckv, kr, g_mla, aq, ak, av, g_ax, wq, wk, wv, g_win, m_mla, m_ax, m_win) = [col(k) for k in range(15)]
    L = w_in.shape[0]
    w = {}
    qkv_cols = [_pair_layout(aq), _pair_layout(ak), av, wq, wk, wv, cq, ckv,
                _pair_layout(jnp.concatenate([kr, kr], axis=-1))]
    gate_cols = [g_mla, g_ax, g_win, m_mla, m_ax, m_win]
    n_qkv = sum(c.shape[-1] for c in qkv_cols)
    pad = jnp.zeros(w_in.shape[:2] + (GATE_COL0 - n_qkv,), w_in.dtype)
    w["in"] = jnp.concatenate(qkv_cols + [pad] + gate_cols, axis=-1).astype(BF16)
    w["n_qkv"] = n_qkv
    w["n_gate"] = sum(c.shape[-1] for c in gate_cols)
    wq4 = w_q_up.reshape(L, Q_LORA, MLA_HEADS, MLA_QK)
    w["q_up"] = jnp.concatenate([wq4[..., :MLA_NOPE].reshape(L, Q_LORA, -1),
                                 _pair_layout(wq4[..., MLA_NOPE:].reshape(L, Q_LORA, -1))], axis=-1).astype(BF16)
    wkv4 = w_kv_up.reshape(L, KV_LORA, MLA_HEADS, MLA_NOPE + MLA_V)
    w["kv_up"] = jnp.concatenate([wkv4[..., :MLA_NOPE].reshape(L, KV_LORA, -1),
                                  wkv4[..., MLA_NOPE:].reshape(L, KV_LORA, -1)], axis=-1).astype(BF16)
    qs_mla = MLA_QK ** -0.5 * LOG2E
    pair_gain = lambda g: jnp.concatenate(
        [g[:, :MLA_NOPE], _pair_layout(jnp.concatenate([g[:, MLA_NOPE:], g[:, MLA_NOPE:]], axis=-1))], axis=-1)
    w["gq"] = (pair_gain(mla_qn) * qs_mla).reshape(L, 1, 2 * LANES)
    w["gk"] = pair_gain(mla_kn).reshape(L, 1, 2 * LANES)
    ones = jnp.ones((L, AX_KV * AX_HD), F32)
    w["qkv_gain"] = jnp.concatenate(
        [jnp.tile(_pair_layout(ax_qn), (1, AX_HEADS)) * (AX_HD ** -0.5 * LOG2E),
         jnp.tile(_pair_layout(ax_kn), (1, AX_KV)), ones,
         jnp.tile(win_qn, (1, WIN_HEADS)) * (WIN_HD ** -0.5 * LOG2E), jnp.tile(win_kn, (1, WIN_KV)), ones,
         q_a_norm, kv_a_norm], axis=-1).reshape(L, 1, -1)
    w["br"] = [w_br_mla.astype(BF16), w_br_ax.astype(BF16), w_br_win.astype(BF16)]
    w["out"] = w_out.astype(BF16)
    return w


def _layer(x, h, mod, mod_next, l, ln_g, w, tabs, bias_tab, sink_tab):
    B, S, D = x.shape
    T = B * S
    h2 = h.reshape(T, D)
    cos_m, sin_m, cos_a, sin_a = tabs[S]
    qkv, kr2 = _qkv_proj(h2, w["in"][l], w["n_qkv"], w["qkv_gain"][l], cos_a, sin_a, S)
    q_m, k_m, v_m = _mla_prep(qkv, 2 * LANES * len(QKV_KINDS), kr2, w["q_up"][l], w["kv_up"][l],
                              w["gq"][l], w["gk"][l], cos_m, sin_m, S)
    gate = _gate_proj(h2, w["in"][l], w["n_gate"])
    r3 = lambda a: a.reshape(B, S, a.shape[-1])
    q_m, k_m, v_m, qkv3, gate3 = r3(q_m), r3(k_m), r3(v_m), r3(qkv), r3(gate)
    a_mla = _flash_attn(q_m, 0, k_m, 0, v_m, 0, gate3, 0,
                        B=B, S=S, H=MLA_HEADS, KV=MLA_HEADS, dk=2 * LANES, rows=min(FLASH_ROWS, S))
    gq = AX_HEADS // AX_KV
    a_ax = _flash_attn(qkv3, 0, qkv3, AX_HEADS, qkv3, AX_HEADS + AX_KV, gate3, AX_KV,
                       B=B, S=S, H=AX_HEADS, KV=AX_KV, dk=AX_HD, rows=min(FLASH_ROWS, S))
    wbase = AX_HEADS + 2 * AX_KV
    a_win = _window_attn(qkv3, wbase // gq, wbase + WIN_HEADS, wbase + WIN_HEADS + WIN_KV,
                         gate3, 2 * AX_KV, bias_tab, sink_tab[l], B=B, S=S)
    a_list = [a.reshape(T, a.shape[-1]) for a in (a_mla, a_ax, a_win)]
    y = _branch_merge(a_list, [wb[l] for wb in w["br"]], gate, 3 * MLA_HEADS * MLA_V)
    if mod_next is None:
        return _out_proj(y.reshape(B, S, D), w["out"][l], x, mod)
    return _out_proj(y.reshape(B, S, D), w["out"][l], x, mod, ln_g[l + 1], mod_next)


def kernel(x_prompt, x_sample, c_prompt, c_sample, ln_g, w_ada, b_ada, w_in, q_a_norm, w_q_up, kv_a_norm, w_kv_up, mla_qn, mla_kn, ax_qn, ax_kn, win_qn, win_kn, win_sink, rel_bias, w_br_mla, w_br_ax, w_br_win, w_out):
    L = ln_g.shape[0]
    D = x_prompt.shape[-1]
    Bp = x_prompt.shape[0]
    w = _prep_weights(D, w_in, w_q_up, w_kv_up, q_a_norm, kv_a_norm, mla_qn, mla_kn,
                      ax_qn, ax_kn, win_qn, win_kn, w_br_mla, w_br_ax, w_br_win, w_out)
    mod_all = _modulation(jnp.concatenate([c_prompt, c_sample], axis=0), w_ada, b_ada)
    tabs = {}
    for S in {x_prompt.shape[1], x_sample.shape[1]}:
        pos = jnp.arange(S, dtype=F32)
        row = jnp.repeat(jnp.arange(S // GRID_W, dtype=F32), GRID_W)
        colp = jnp.tile(jnp.arange(GRID_W, dtype=F32), S // GRID_W)
        tabs[S] = _rope_tables(pos, pos) + _rope_tables(row, colp)
    bias_tab = _window_bias_table(rel_bias)
    G = WIN_HEADS // WIN_KV
    sink_tab = jnp.repeat(win_sink.astype(F32) * LOG2E, BLOCK, axis=-1).reshape(L, WIN_KV, G * BLOCK, 1)
    mods_p = [mod_all[l, :Bp][:, None, :] for l in range(L)]
    mods_s = [mod_all[l, Bp:][:, None, :] for l in range(L)]
    yp, ys = x_prompt, x_sample
    hp = _norm_modulate(yp, ln_g[0], mods_p[0])
    hs = _norm_modulate(ys, ln_g[0], mods_s[0])
    for l in range(L):
        nxt = l + 1 < L
        yp, hp = _layer(yp, hp, mods_p[l], mods_p[l + 1] if nxt else None, l, ln_g, w, tabs, bias_tab, sink_tab)
        ys, hs = _layer(ys, hs, mods_s[l], mods_s[l + 1] if nxt else None, l, ln_g, w, tabs, bias_tab, sink_tab)
    return (yp, ys)
```

```python
import functools
import math

import jax
import jax.numpy as jnp
from jax import lax
from jax.experimental import pallas as pl
from jax.experimental.pallas import tpu as pltpu

F32 = jnp.float32
BF16 = jnp.bfloat16

GRID_W = 64
BLOCK = 128
EPS = 1e-6
ROPE_THETA = 10000.0
NEG = -1e30
LOG2E = 1.4426950408889634
LANES = 128

MLA_HEADS = 8
MLA_NOPE = 128
MLA_ROPE = 64
MLA_QK = MLA_NOPE + MLA_ROPE
MLA_V = 128
Q_LORA = 512
KV_LORA = 256
AX_HEADS = 8
AX_KV = 2
AX_HD = 128
WIN_HEADS = 8
WIN_KV = 2
WIN_HD = 128
WINDOW = 128
N_BUCKETS = 32
MAX_DIST = 128
ROPE_HALF = 32

VMEM_LIMIT = 56 * 1024 * 1024
GATE_COL0 = 4096

MOD_TN = 512
NORM_TS = 512
QKV_TM, QKV_RC = 1024, 256
MLA_PREP_TM, MLA_PREP_RC = 1024, 128
GATE_TM, GATE_TN = 2048, 512
FLASH_ROWS, FLASH_TK, FLASH_NT = 1024, 2048, 4
WIN_NQ = (32, 16, 4, 1)
BR_TM, BR_TN = 512, 1024
OUT_TM, OUT_RC = 512, 256


def _cparams(sem):
    return pltpu.CompilerParams(dimension_semantics=sem, vmem_limit_bytes=VMEM_LIMIT)


def _pick(n, pref):
    t = min(pref, n)
    while n % t:
        t //= 2
    return t


def _mod_kernel(c_ref, w_ref, b_ref, o_ref):
    o_ref[0] = jnp.dot(c_ref[...], w_ref[0].astype(BF16), preferred_element_type=F32) + b_ref[0]


def _modulation(c_all, w_ada, b_ada):
    L, D, N = w_ada.shape
    R = c_all.shape[0]
    tn = _pick(N, MOD_TN)
    return pl.pallas_call(
        _mod_kernel,
        grid=(L, N // tn),
        in_specs=[pl.BlockSpec((R, D), lambda l, j: (0, 0)),
                  pl.BlockSpec((1, D, tn), lambda l, j: (l, 0, j)),
                  pl.BlockSpec((1, 1, tn), lambda l, j: (l, 0, j))],
        out_specs=pl.BlockSpec((1, R, tn), lambda l, j: (l, 0, j)),
        out_shape=jax.ShapeDtypeStruct((L, R, N), F32),
        compiler_params=_cparams(("parallel", "parallel")),
        name="adaln_mod",
    )(c_all.astype(BF16), w_ada, b_ada.reshape(L, 1, N))


def _h_kernel(x_ref, g_ref, sh_ref, sc_ref, o_ref):
    x = x_ref[0]
    r = lax.rsqrt(jnp.mean(x * x, axis=-1, keepdims=True) + EPS)
    y = x * r * g_ref[...]
    o_ref[0] = (y * (1.0 + sc_ref[0]) + sh_ref[0]).astype(BF16)


def _norm_modulate(x, ln_g, mod):
    B, S, D = x.shape
    ts = _pick(S, NORM_TS)
    return pl.pallas_call(
        _h_kernel,
        grid=(B, S // ts),
        in_specs=[pl.BlockSpec((1, ts, D), lambda b, i: (b, i, 0)),
                  pl.BlockSpec((1, D), lambda b, i: (0, 0)),
                  pl.BlockSpec((1, 1, D), lambda b, i: (b, 0, 0)),
                  pl.BlockSpec((1, 1, D), lambda b, i: (b, 0, 1))],
        out_specs=pl.BlockSpec((1, ts, D), lambda b, i: (b, i, 0)),
        out_shape=jax.ShapeDtypeStruct((B, S, D), BF16),
        compiler_params=_cparams(("parallel", "parallel")),
        name="norm_modulate",
    )(x, ln_g.reshape(1, D), mod, mod)


def _pair_layout(x):
    shp = x.shape
    x = x.reshape(shp[:-1] + (shp[-1] // LANES, 2, 2, ROPE_HALF))
    return jnp.swapaxes(x, -3, -2).reshape(shp)


def _rope128(y, cos, sin_signed):
    return y * cos + pltpu.roll(y, LANES // 2, 1) * sin_signed


def _rope_tables(pos_a, pos_b):
    freqs = ROPE_THETA ** (-jnp.arange(ROPE_HALF, dtype=F32) / ROPE_HALF)
    ang_a = pos_a[:, None] * freqs[None, :]
    ang_b = pos_b[:, None] * freqs[None, :]
    cos = jnp.concatenate([jnp.cos(ang_a), jnp.cos(ang_b)], axis=-1)
    sin = jnp.concatenate([jnp.sin(ang_a), jnp.sin(ang_b)], axis=-1)
    return jnp.concatenate([cos, cos], axis=-1), jnp.concatenate([-sin, sin], axis=-1)


def _mla_prep_kernel(cqn_ref, ckvn_ref, kr_ref, wq_ref, wkv_ref, gq_ref, gk_ref,
                     cos_ref, sin_ref, q_ref, k_ref, v_ref, *, rc):
    for r in range(cqn_ref.shape[0] // rc):
        _mla_prep_rows(slice(rc * r, rc * (r + 1)), cqn_ref, ckvn_ref, kr_ref, wq_ref, wkv_ref,
                       gq_ref, gk_ref, cos_ref, sin_ref, q_ref, k_ref, v_ref)


def _mla_prep_rows(rows, cqn_ref, ckvn_ref, kr_ref, wq_ref, wkv_ref, gq_ref, gk_ref,
                   cos_ref, sin_ref, q_ref, k_ref, v_ref):
    kr2 = kr_ref[rows, :]
    qraw = jnp.dot(cqn_ref[rows, :], wq_ref[...], preferred_element_type=F32)
    kvraw = jnp.dot(ckvn_ref[rows, :], wkv_ref[...], preferred_element_type=F32)
    cos = cos_ref[rows, :]
    sin = sin_ref[rows, :]
    lane = lax.broadcasted_iota(jnp.int32, cos.shape, 1)
    even = (lane % (2 * ROPE_HALF)) < ROPE_HALF
    gq = gq_ref[...]
    gk = gk_ref[...]
    nope_w = MLA_HEADS * MLA_NOPE
    inv_qk = 1.0 / MLA_QK
    kr_sq = kr2 * kr2
    k_rope = _rope128(kr2 * gk[:, LANES:], cos, sin)
    for p in range(MLA_HEADS // 2):
        qr = qraw[:, nope_w + LANES * p: nope_w + LANES * (p + 1)]
        qr_sq = qr * qr
        q_rope = _rope128(qr * gq[:, LANES:], cos, sin)
        for e in range(2):
            hd = 2 * p + e
            keep = even if e == 0 else jnp.logical_not(even)
            qn = qraw[:, LANES * hd: LANES * (hd + 1)]
            rq = lax.rsqrt(jnp.sum(qn * qn + jnp.where(keep, qr_sq, 0.0), axis=-1, keepdims=True) * inv_qk + EPS)
            q_ref[rows, 2 * LANES * hd: 2 * LANES * hd + LANES] = (qn * rq * gq[:, :LANES]).astype(BF16)
            q_ref[rows, 2 * LANES * hd + LANES: 2 * LANES * (hd + 1)] = jnp.where(keep, q_rope * rq, 0.0).astype(BF16)
            kn = kvraw[:, LANES * hd: LANES * (hd + 1)]
            rk = lax.rsqrt(jnp.sum(kn * kn + jnp.where(keep, kr_sq, 0.0), axis=-1, keepdims=True) * inv_qk + EPS)
            k_ref[rows, 2 * LANES * hd: 2 * LANES * hd + LANES] = (kn * rk * gk[:, :LANES]).astype(BF16)
            k_ref[rows, 2 * LANES * hd + LANES: 2 * LANES * (hd + 1)] = jnp.where(keep, k_rope * rk, 0.0).astype(BF16)
    v_ref[rows, :] = kvraw[:, nope_w:].astype(BF16)


def _mla_prep(lat, lat_col0, kr2, wq, wkv, gq, gk, cos, sin, S):
    T = lat.shape[0]
    tm = _pick(S, MLA_PREP_TM)
    nS = S // tm
    hq = MLA_HEADS * 2 * LANES
    const = lambda i: (0, 0)
    return pl.pallas_call(
        functools.partial(_mla_prep_kernel, rc=min(MLA_PREP_RC, tm)),
        grid=(T // tm,),
        in_specs=[pl.BlockSpec((tm, Q_LORA), lambda i: (i, lat_col0 // Q_LORA)),
                  pl.BlockSpec((tm, KV_LORA), lambda i: (i, (lat_col0 + Q_LORA) // KV_LORA)),
                  pl.BlockSpec((tm, LANES), lambda i: (i, 0)),
                  pl.BlockSpec(wq.shape, const),
                  pl.BlockSpec(wkv.shape, const),
                  pl.BlockSpec((1, 2 * LANES), const),
                  pl.BlockSpec((1, 2 * LANES), const),
                  pl.BlockSpec((tm, LANES), lambda i: (i % nS, 0)),
                  pl.BlockSpec((tm, LANES), lambda i: (i % nS, 0))],
        out_specs=[pl.BlockSpec((tm, hq), lambda i: (i, 0)),
                   pl.BlockSpec((tm, hq), lambda i: (i, 0)),
                   pl.BlockSpec((tm, MLA_HEADS * MLA_V), lambda i: (i, 0))],
        out_shape=[jax.ShapeDtypeStruct((T, hq), BF16),
                   jax.ShapeDtypeStruct((T, hq), BF16),
                   jax.ShapeDtypeStruct((T, MLA_HEADS * MLA_V), BF16)],
        compiler_params=_cparams(("parallel",)),
        name="mla_prep",
    )(lat, lat, kr2, wq, wkv, gq, gk, cos, sin)


QKV_KINDS = (0, 0, 0, 0, 0, 2, 1, 1, 1, 1, 1, 2)


def _qkv_kernel(h_ref, w_ref, gain_ref, cos_ref, sin_ref, o_ref, kr_ref, *, rc):
    tm = h_ref.shape[0]
    tn = 2 * LANES
    c0 = tn * len(QKV_KINDS)
    for r in range(tm // rc):
        rows = slice(rc * r, rc * (r + 1))
        for lo, width in ((c0, Q_LORA), (c0 + Q_LORA, KV_LORA)):
            x = jnp.dot(h_ref[rows, :], w_ref[:, lo: lo + width], preferred_element_type=F32)
            y = x * lax.rsqrt(jnp.mean(x * x, axis=-1, keepdims=True) + EPS) * gain_ref[:, lo: lo + width]
            o_ref[rows, lo: lo + width] = y.astype(BF16)
        kr_ref[rows, :] = jnp.dot(h_ref[rows, :], w_ref[:, c0 + Q_LORA + KV_LORA:], preferred_element_type=F32)
        for t, kind in enumerate(QKV_KINDS):
            acc = jnp.dot(h_ref[rows, :], w_ref[:, tn * t: tn * (t + 1)], preferred_element_type=F32)
            if kind == 2:
                o_ref[rows, tn * t: tn * (t + 1)] = acc.astype(BF16)
                continue
            for e in range(2):
                lanes = slice(tn * t + LANES * e, tn * t + LANES * (e + 1))
                x = acc[:, LANES * e: LANES * (e + 1)]
                y = x * lax.rsqrt(jnp.mean(x * x, axis=-1, keepdims=True) + EPS) * gain_ref[:, lanes]
                if kind == 0:
                    y = _rope128(y, cos_ref[rows, :], sin_ref[rows, :])
                o_ref[rows, lanes] = y.astype(BF16)


def _qkv_proj(h2, w, N, gains, cos, sin, S):
    T, D = h2.shape
    No = N - LANES
    tm = _pick(S, QKV_TM)
    nS = S // tm
    return pl.pallas_call(
        functools.partial(_qkv_kernel, rc=min(QKV_RC, tm)),
        grid=(T // tm,),
        in_specs=[pl.BlockSpec((tm, D), lambda i: (i, 0)),
                  pl.BlockSpec((D, N), lambda i: (0, 0), pipeline_mode=pl.Buffered(1)),
                  pl.BlockSpec((1, No), lambda i: (0, 0)),
                  pl.BlockSpec((tm, LANES), lambda i: (i % nS, 0)),
                  pl.BlockSpec((tm, LANES), lambda i: (i % nS, 0))],
        out_specs=[pl.BlockSpec((tm, No), lambda i: (i, 0)),
                   pl.BlockSpec((tm, LANES), lambda i: (i, 0))],
        out_shape=[jax.ShapeDtypeStruct((T, No), BF16),
                   jax.ShapeDtypeStruct((T, LANES), F32)],
        compiler_params=_cparams(("parallel",)),
        name="qkv_proj",
    )(h2, w, gains, cos, sin)


def _gate_kernel(h_ref, w_ref, o_ref):
    o_ref[...] = jnp.dot(h_ref[...], w_ref[...], preferred_element_type=F32)


def _gate_proj(h2, w, N):
    T, D = h2.shape
    tm = _pick(T, GATE_TM)
    tn = _pick(N, GATE_TN)
    j0 = GATE_COL0 // tn
    return pl.pallas_call(
        _gate_kernel,
        grid=(T // tm, N // tn),
        in_specs=[pl.BlockSpec((tm, D), lambda i, j: (i, 0)),
                  pl.BlockSpec((D, tn), lambda i, j: (0, j0 + j))],
        out_specs=pl.BlockSpec((tm, tn), lambda i, j: (i, j)),
        out_shape=jax.ShapeDtypeStruct((T, N), F32),
        compiler_params=_cparams(("parallel", "arbitrary")),
        name="gate_proj",
    )(h2, w)


def _flash_kernel(q_ref, k_ref, v_ref, g_ref, o_ref, vext_ref, *, G, dk, tq, tk, S, nt):
    qi = pl.program_id(2)

    def build_vext():
        lane = lax.broadcasted_iota(jnp.int32, (S, LANES), 1)
        vext_ref[:, :LANES] = v_ref[0]
        vext_ref[:, LANES:] = jnp.where(lane == 0, 1.0, 0.0).astype(BF16)

    if nt * tq == S:
        build_vext()
    else:
        pl.when(qi == 0)(build_vext)
    n = S // tk
    for t in range(nt):
        qrows = slice(tq * t, tq * (t + 1))
        if G == 1:
            q = q_ref[0, qrows, :]
        else:
            q = jnp.concatenate([q_ref[0, qrows, dk * g: dk * (g + 1)] for g in range(G)], axis=0)

        def scores(c, q=q):
            k = k_ref[0, tk * c: tk * (c + 1), :]
            return lax.dot_general(q, k, (((1,), (1,)), ((), ())), preferred_element_type=F32)

        s_next = scores(0)
        m = None
        acc = None
        for c in range(n):
            s = s_next
            if c + 1 < n:
                s_next = scores(c + 1)
            m_blk = jnp.max(s, axis=1, keepdims=True)
            m_new = m_blk if m is None else jnp.maximum(m, m_blk)
            p = jnp.exp2(s - m_new).astype(BF16)
            pv = jnp.dot(p, vext_ref[tk * c: tk * (c + 1), :], preferred_element_type=F32)
            acc = pv if acc is None else jnp.exp2(m - m_new) * acc + pv
            m = m_new
        o = acc[:, :LANES] / acc[:, LANES:LANES + 1]
        for g in range(G):
            gate = g_ref[0, qrows, LANES * g: LANES * (g + 1)]
            o_ref[0, qrows, LANES * g: LANES * (g + 1)] = (
                o[tq * g: tq * (g + 1)] * (gate * jax.nn.sigmoid(gate))).astype(BF16)


def _flash_attn(q_arr, q_blk0, k_arr, k_blk0, v_arr, v_blk0, g_arr, g_blk0, *, B, S, H, KV, dk, rows):
    G = H // KV
    tq = rows // G
    tk = _pick(S // 2, FLASH_TK)
    nt = min(FLASH_NT, S // tq)
    kern = functools.partial(_flash_kernel, G=G, dk=dk, tq=tq, tk=tk, S=S, nt=nt)
    return pl.pallas_call(
        kern,
        grid=(B, KV, S // (tq * nt)),
        in_specs=[pl.BlockSpec((1, tq * nt, G * dk), lambda b, h, i: (b, i, q_blk0 + h)),
                  pl.BlockSpec((1, S, dk), lambda b, h, i: (b, 0, k_blk0 + h)),
                  pl.BlockSpec((1, S, LANES), lambda b, h, i: (b, 0, v_blk0 + h)),
                  pl.BlockSpec((1, tq * nt, G * LANES), lambda b, h, i: (b, i, g_blk0 + h))],
        out_specs=pl.BlockSpec((1, tq * nt, G * LANES), lambda b, h, i: (b, i, h)),
        out_shape=jax.ShapeDtypeStruct((B, S, H * LANES), BF16),
        scratch_shapes=[pltpu.VMEM((S, 2 * LANES), BF16)],
        compiler_params=_cparams(("parallel", "parallel", "arbitrary")),
        name=f"flash_attn_g{G}",
    )(q_arr, k_arr, v_arr, g_arr)


def _win_kernel(q_ref, kp_ref, kc_ref, kn_ref, vp_ref, vc_ref, vn_ref, bias_ref, sink_ref, g_ref, o_ref, *, G, nq, nsteps):
    j = pl.program_id(2)
    nt = (((1,), (1,)), ((), ()))
    sink = sink_ref[0]

    def kv_block(prev_ref, cur_ref, next_ref, t):
        if t < 0:
            return prev_ref[0]
        if t >= nq:
            return next_ref[0]
        return cur_ref[0, BLOCK * t: BLOCK * (t + 1), :]

    for t in range(nq):
        rows = slice(BLOCK * t, BLOCK * (t + 1))
        q = jnp.concatenate([q_ref[0, rows, LANES * g: LANES * (g + 1)] for g in range(G)], axis=0)
        s = [lax.dot_general(q, kv_block(kp_ref, kc_ref, kn_ref, t - 1 + d), nt, preferred_element_type=F32)
             + bias_ref[0, :, BLOCK * d: BLOCK * (d + 1)] for d in range(3)]
        if t == 0:
            s[0] = s[0] + jnp.where(j == 0, NEG, 0.0)
        if t == nq - 1:
            s[2] = s[2] + jnp.where(j == nsteps - 1, NEG, 0.0)
        m = jnp.maximum(jnp.max(jnp.maximum(jnp.maximum(s[0], s[1]), s[2]), axis=1, keepdims=True), sink)
        p = [jnp.exp2(sd - m) for sd in s]
        l = jnp.sum(p[0] + p[1] + p[2], axis=1, keepdims=True) + jnp.exp2(sink - m)
        o = sum(jnp.dot(p[d].astype(BF16), kv_block(vp_ref, vc_ref, vn_ref, t - 1 + d), preferred_element_type=F32)
                for d in range(3)) / l
        for g in range(G):
            gate = g_ref[0, rows, LANES * g: LANES * (g + 1)]
            o_ref[0, rows, LANES * g: LANES * (g + 1)] = (
                o[BLOCK * g: BLOCK * (g + 1)] * (gate * jax.nn.sigmoid(gate))).astype(BF16)


def _window_attn(qkv, q_blk0, k_blk0, v_blk0, gate, g_blk0, bias, sink, *, B, S):
    G = WIN_HEADS // WIN_KV
    nb = S // BLOCK
    nq = next(n for n in WIN_NQ if nb % n == 0)
    nsteps = nb // nq
    kern = functools.partial(_win_kernel, G=G, nq=nq, nsteps=nsteps)

    def kv_specs(blk0):
        return [pl.BlockSpec((1, BLOCK, LANES), lambda b, h, j: (b, jnp.maximum(j * nq - 1, 0), blk0 + h)),
                pl.BlockSpec((1, nq * BLOCK, LANES), lambda b, h, j: (b, j, blk0 + h)),
                pl.BlockSpec((1, BLOCK, LANES), lambda b, h, j: (b, jnp.minimum((j + 1) * nq, nb - 1), blk0 + h))]

    return pl.pallas_call(
        kern,
        grid=(B, WIN_KV, nsteps),
        in_specs=[pl.BlockSpec((1, nq * BLOCK, G * LANES), lambda b, h, j: (b, j, q_blk0 + h))]
        + kv_specs(k_blk0) + kv_specs(v_blk0)
        + [pl.BlockSpec((1, G * BLOCK, 3 * BLOCK), lambda b, h, j: (h, 0, 0)),
           pl.BlockSpec((1, G * BLOCK, 1), lambda b, h, j: (h, 0, 0)),
           pl.BlockSpec((1, nq * BLOCK, G * LANES), lambda b, h, j: (b, j, g_blk0 + h))],
        out_specs=pl.BlockSpec((1, nq * BLOCK, G * LANES), lambda b, h, j: (b, j, h)),
        out_shape=jax.ShapeDtypeStruct((B, S, WIN_HEADS * LANES), BF16),
        compiler_params=_cparams(("parallel", "parallel", "arbitrary")),
        name="window_attn",
    )(qkv, qkv, qkv, qkv, qkv, qkv, qkv, bias, sink, gate)


def _t5_bucket(rel):
    half = N_BUCKETS // 2
    max_exact = half // 2
    ret = jnp.where(rel > 0, half, 0)
    n = jnp.abs(rel)
    nf = jnp.maximum(n, 1).astype(F32)
    large = max_exact + (jnp.log(nf / max_exact) / math.log(MAX_DIST / max_exact)
                         * (half - max_exact)).astype(jnp.int32)
    large = jnp.minimum(large, half - 1)
    return ret + jnp.where(n < max_exact, n, large)


def _bias_kernel(bucket_ref, rb_ref, o_ref):
    bucket = bucket_ref[...]
    row = lax.broadcasted_iota(jnp.int32, bucket.shape, 0)
    colk = lax.broadcasted_iota(jnp.int32, bucket.shape, 1)
    inband = jnp.abs(colk - WINDOW - row) <= WINDOW
    for h in range(WIN_HEADS):
        acc = jnp.zeros(bucket.shape, F32)
        for b in range(N_BUCKETS):
            acc = jnp.where(bucket == b, rb_ref[b, h], acc)
        o_ref[h] = jnp.where(inband, acc * LOG2E, NEG)


def _window_bias_table(rel_bias):
    span = BLOCK + 2 * WINDOW
    rel = jnp.arange(span)[None, :] - WINDOW - jnp.arange(BLOCK)[:, None]
    bias = pl.pallas_call(
        _bias_kernel,
        in_specs=[pl.BlockSpec(memory_space=pltpu.VMEM), pl.BlockSpec(memory_space=pltpu.SMEM)],
        out_specs=pl.BlockSpec(memory_space=pltpu.VMEM),
        out_shape=jax.ShapeDtypeStruct((WIN_HEADS, BLOCK, span), F32),
        name="t5_bias_table",
    )(_t5_bucket(rel).astype(jnp.int32), rel_bias.astype(F32))
    G = WIN_HEADS // WIN_KV
    return bias.reshape(WIN_KV, G * BLOCK, span)


def _branch_kernel(*refs, tn):
    a, w, m, o_ref = refs[:3], refs[3:6], refs[6:-1], refs[-1]
    nj = o_ref.shape[1] // tn
    for j in range(nj):
        cols = slice(tn * j, tn * (j + 1))
        y = None
        for k in range(3):
            t = jax.nn.sigmoid(m[nj * k + j][...]) * jnp.dot(a[k][...], w[k][:, cols], preferred_element_type=F32)
            y = t if y is None else y + t
        o_ref[:, cols] = y.astype(BF16)


def _branch_merge(a_list, w_list, gate, m_col0):
    T, W = a_list[0].shape
    D = w_list[0].shape[1]
    tm = _pick(T, BR_TM)
    tn = _pick(D, BR_TN)
    nj = D // tn
    a_spec = pl.BlockSpec((tm, W), lambda i: (i, 0))
    w_spec = pl.BlockSpec((W, D), lambda i: (0, 0), pipeline_mode=pl.Buffered(1))

    def m_spec(k, j):
        blk = (m_col0 + k * D) // tn + j
        return pl.BlockSpec((tm, tn), lambda i: (i, blk))

    assert m_col0 % tn == 0
    m_specs = [m_spec(k, j) for k in range(3) for j in range(nj)]
    return pl.pallas_call(
        functools.partial(_branch_kernel, tn=tn),
        grid=(T // tm,),
        in_specs=[a_spec] * 3 + [w_spec] * 3 + m_specs,
        out_specs=pl.BlockSpec((tm, D), lambda i: (i, 0)),
        out_shape=jax.ShapeDtypeStruct((T, D), BF16),
        compiler_params=_cparams(("parallel",)),
        name="branch_merge",
    )(*a_list, *w_list, *([gate] * (3 * nj)))


def _out_kernel(y_ref, w_ref, x_ref, gt_ref, *rest, rc, with_next):
    if with_next:
        g_ref, sh_ref, sc_ref, o_ref, hn_ref = rest
    else:
        (o_ref,) = rest
    for r in range(y_ref.shape[1] // rc):
        rows = slice(rc * r, rc * (r + 1))
        x = x_ref[0, rows, :] + gt_ref[0] * jnp.dot(y_ref[0, rows, :], w_ref[...], preferred_element_type=F32)
        o_ref[0, rows, :] = x
        if with_next:
            rms = lax.rsqrt(jnp.mean(x * x, axis=-1, keepdims=True) + EPS)
            hn_ref[0, rows, :] = ((x * rms * g_ref[...]) * (1.0 + sc_ref[0]) + sh_ref[0]).astype(BF16)


def _out_proj(y, w_out, x, mod, ln_g_next=None, mod_next=None):
    B, S, D = x.shape
    tm = _pick(S, OUT_TM)
    with_next = ln_g_next is not None
    row_spec = pl.BlockSpec((1, tm, D), lambda b, i: (b, i, 0))
    in_specs = [row_spec,
                pl.BlockSpec((D, D), lambda b, i: (0, 0)),
                row_spec,
                pl.BlockSpec((1, 1, D), lambda b, i: (b, 0, 2))]
    args = [y, w_out, x, mod]
    out_specs = [row_spec]
    out_shape = [jax.ShapeDtypeStruct((B, S, D), F32)]
    if with_next:
        in_specs += [pl.BlockSpec((1, D), lambda b, i: (0, 0)),
                     pl.BlockSpec((1, 1, D), lambda b, i: (b, 0, 0)),
                     pl.BlockSpec((1, 1, D), lambda b, i: (b, 0, 1))]
        args += [ln_g_next.reshape(1, D), mod_next, mod_next]
        out_specs.append(row_spec)
        out_shape.append(jax.ShapeDtypeStruct((B, S, D), BF16))
    res = pl.pallas_call(
        functools.partial(_out_kernel, rc=min(OUT_RC, tm), with_next=with_next),
        grid=(B, S // tm),
        in_specs=in_specs,
        out_specs=out_specs,
        out_shape=out_shape,
        compiler_params=_cparams(("parallel", "parallel")),
        name="out_proj",
    )(*args)
    return (res[0], res[1]) if with_next else (res[0], None)


def _split_points(D):
    sizes = [Q_LORA, KV_LORA, MLA_ROPE, MLA_HEADS * MLA_V,
             AX_HEADS * AX_HD, AX_KV * AX_HD, AX_KV * AX_HD, AX_HEADS * AX_HD,
             WIN_HEADS * WIN_HD, WIN_KV * WIN_HD, WIN_KV * WIN_HD, WIN_HEADS * WIN_HD,
             D, D, D]
    pts, acc = [0], 0
    for s in sizes:
        acc += s
        pts.append(acc)
    return pts


def _prep_weights(D, w_in, w_q_up, w_kv_up, q_a_norm, kv_a_norm, mla_qn, mla_kn,
                  ax_qn, ax_kn, win_qn, win_kn, w_br_mla, w_br_ax, w_br_win, w_out):
    p = _split_points(D)
    col = lambda k: w_in[:, :, p[k]:p[k + 1]]
    (cq, ckv, kr, g_mla, aq, ak, av, g_ax, wq, wk, wv, g_win, m_mla, m_ax, m_win) = [col(k) for k in range(15)]
    L = w_in.shape[0]
    w = {}
    qkv_cols = [_pair_layout(aq), _pair_layout(ak), av, wq, wk, wv, cq, ckv,
                _pair_layout(jnp.concatenate([kr, kr], axis=-1))]
    gate_cols = [g_mla, g_ax, g_win, m_mla, m_ax, m_win]
    n_qkv = sum(c.shape[-1] for c in qkv_cols)
    pad = jnp.zeros(w_in.shape[:2] + (GATE_COL0 - n_qkv,), w_in.dtype)
    w["in"] = jnp.concatenate(qkv_cols + [pad] + gate_cols, axis=-1).astype(BF16)
    w["n_qkv"] = n_qkv
    w["n_gate"] = sum(c.shape[-1] for c in gate_cols)
    wq4 = w_q_up.reshape(L, Q_LORA, MLA_HEADS, MLA_QK)
    w["q_up"] = jnp.concatenate([wq4[..., :MLA_NOPE].reshape(L, Q_LORA, -1),
                                 _pair_layout(wq4[..., MLA_NOPE:].reshape(L, Q_LORA, -1))], axis=-1).astype(BF16)
    wkv4 = w_kv_up.reshape(L, KV_LORA, MLA_HEADS, MLA_NOPE + MLA_V)
    w["kv_up"] = jnp.concatenate([wkv4[..., :MLA_NOPE].reshape(L, KV_LORA, -1),
                                  wkv4[..., MLA_NOPE:].reshape(L, KV_LORA, -1)], axis=-1).astype(BF16)
    qs_mla = MLA_QK ** -0.5 * LOG2E
    pair_gain = lambda g: jnp.concatenate(
        [g[:, :MLA_NOPE], _pair_layout(jnp.concatenate([g[:, MLA_NOPE:], g[:, MLA_NOPE:]], axis=-1))], axis=-1)
    w["gq"] = (pair_gain(mla_qn) * qs_mla).reshape(L, 1, 2 * LANES)
    w["gk"] = pair_gain(mla_kn).reshape(L, 1, 2 * LANES)
    ones = jnp.ones((L, AX_KV * AX_HD), F32)
    w["qkv_gain"] = jnp.concatenate(
        [jnp.tile(_pair_layout(ax_qn), (1, AX_HEADS)) * (AX_HD ** -0.5 * LOG2E),
         jnp.tile(_pair_layout(ax_kn), (1, AX_KV)), ones,
         jnp.tile(win_qn, (1, WIN_HEADS)) * (WIN_HD ** -0.5 * LOG2E), jnp.tile(win_kn, (1, WIN_KV)), ones,
         q_a_norm, kv_a_norm], axis=-1).reshape(L, 1, -1)
    w["br"] = [w_br_mla.astype(BF16), w_br_ax.astype(BF16), w_br_win.astype(BF16)]
    w["out"] = w_out.astype(BF16)
    return w


def _layer(x, h, mod, mod_next, l, ln_g, w, tabs, bias_tab, sink_tab):
    B, S, D = x.shape
    T = B * S
    h2 = h.reshape(T, D)
    cos_m, sin_m, cos_a, sin_a = tabs[S]
    qkv, kr2 = _qkv_proj(h2, w["in"][l], w["n_qkv"], w["qkv_gain"][l], cos_a, sin_a, S)
    q_m, k_m, v_m = _mla_prep(qkv, 2 * LANES * len(QKV_KINDS), kr2, w["q_up"][l], w["kv_up"][l],
                              w["gq"][l], w["gk"][l], cos_m, sin_m, S)
    gate = _gate_proj(h2, w["in"][l], w["n_gate"])
    r3 = lambda a: a.reshape(B, S, a.shape[-1])
    q_m, k_m, v_m, qkv3, gate3 = r3(q_m), r3(k_m), r3(v_m), r3(qkv), r3(gate)
    a_mla = _flash_attn(q_m, 0, k_m, 0, v_m, 0, gate3, 0,
                        B=B, S=S, H=MLA_HEADS, KV=MLA_HEADS, dk=2 * LANES, rows=min(FLASH_ROWS, S))
    gq = AX_HEADS // AX_KV
    a_ax = _flash_attn(qkv3, 0, qkv3, AX_HEADS, qkv3, AX_HEADS + AX_KV, gate3, AX_KV,
                       B=B, S=S, H=AX_HEADS, KV=AX_KV, dk=AX_HD, rows=min(FLASH_ROWS, S))
    wbase = AX_HEADS + 2 * AX_KV
    a_win = _window_attn(qkv3, wbase // gq, wbase + WIN_HEADS, wbase + WIN_HEADS + WIN_KV,
                         gate3, 2 * AX_KV, bias_tab, sink_tab[l], B=B, S=S)
    a_list = [a.reshape(T, a.shape[-1]) for a in (a_mla, a_ax, a_win)]
    y = _branch_merge(a_list, [wb[l] for wb in w["br"]], gate, 3 * MLA_HEADS * MLA_V)
    if mod_next is None:
        return _out_proj(y.reshape(B, S, D), w["out"][l], x, mod)
    return _out_proj(y.reshape(B, S, D), w["out"][l], x, mod, ln_g[l + 1], mod_next)


def kernel(x_prompt, x_sample, c_prompt, c_sample, ln_g, w_ada, b_ada, w_in, q_a_norm, w_q_up, kv_a_norm, w_kv_up, mla_qn, mla_kn, ax_qn, ax_kn, win_qn, win_kn, win_sink, rel_bias, w_br_mla, w_br_ax, w_br_win, w_out):
    L = ln_g.shape[0]
    D = x_prompt.shape[-1]
    Bp = x_prompt.shape[0]
    w = _prep_weights(D, w_in, w_q_up, w_kv_up, q_a_norm, kv_a_norm, mla_qn, mla_kn,
                      ax_qn, ax_kn, win_qn, win_kn, w_br_mla, w_br_ax, w_br_win, w_out)
    mod_all = _modulation(jnp.concatenate([c_prompt, c_sample], axis=0), w_ada, b_ada)
    tabs = {}
    for S in {x_prompt.shape[1], x_sample.shape[1]}:
        pos = jnp.arange(S, dtype=F32)
        row = jnp.repeat(jnp.arange(S // GRID_W, dtype=F32), GRID_W)
        colp = jnp.tile(jnp.arange(GRID_W, dtype=F32), S // GRID_W)
        tabs[S] = _rope_tables(pos, pos) + _rope_tables(row, colp)
    bias_tab = _window_bias_table(rel_bias)
    G = WIN_HEADS // WIN_KV
    sink_tab = jnp.repeat(win_sink.astype(F32) * LOG2E, BLOCK, axis=-1).reshape(L, WIN_KV, G * BLOCK, 1)
    mods_p = [mod_all[l, :Bp][:, None, :] for l in range(L)]
    mods_s = [mod_all[l, Bp:][:, None, :] for l in range(L)]
    yp, ys = x_prompt, x_sample
    hp = _norm_modulate(yp, ln_g[0], mods_p[0])
    hs = _norm_modulate(ys, ln_g[0], mods_s[0])
    for l in range(L):
        nxt = l + 1 < L
        yp, hp = _layer(yp, hp, mods_p[l], mods_p[l + 1] if nxt else None, l, ln_g, w, tabs, bias_tab, sink_tab)
        ys, hs = _layer(ys, hs, mods_s[l], mods_s[l + 1] if nxt else None, l, ln_g, w, tabs, bias_tab, sink_tab)
    return (yp, ys)
```

```python
import functools
import math

import jax
import jax.numpy as jnp
from jax import lax
from jax.experimental import pallas as pl
from jax.experimental.pallas import tpu as pltpu

F32 = jnp.float32
BF16 = jnp.bfloat16

GRID_W = 64
BLOCK = 128
EPS = 1e-6
ROPE_THETA = 10000.0
NEG = -1e30
LOG2E = 1.4426950408889634
LANES = 128

MLA_HEADS = 8
MLA_NOPE = 128
MLA_ROPE = 64
MLA_QK = MLA_NOPE + MLA_ROPE
MLA_V = 128
Q_LORA = 512
KV_LORA = 256
AX_HEADS = 8
AX_KV = 2
AX_HD = 128
WIN_HEADS = 8
WIN_KV = 2
WIN_HD = 128
WINDOW = 128
N_BUCKETS = 32
MAX_DIST = 128
ROPE_HALF = 32

VMEM_LIMIT = 56 * 1024 * 1024

MOD_TN = 512
NORM_TS = 512
QKV_TM, QKV_RC = 1024, 256
MLA_PREP_TM, MLA_PREP_RC = 1024, 128
GATE_TM, GATE_TN, GATE_RC, GATE_GROUPS = 512, 512, 256, 3
FLASH_ROWS, FLASH_TK, FLASH_NT = 1024, 2048, 4
WIN_NQ = (32, 16, 4, 1)
BR_TM, BR_TN = 512, 1024
OUT_TM, OUT_RC = 512, 256


def _cparams(sem):
    return pltpu.CompilerParams(dimension_semantics=sem, vmem_limit_bytes=VMEM_LIMIT)


def _pick(n, pref):
    t = min(pref, n)
    while n % t:
        t //= 2
    return t


def _mod_kernel(c_ref, w_ref, b_ref, o_ref):
    o_ref[0] = jnp.dot(c_ref[...], w_ref[0].astype(BF16), preferred_element_type=F32) + b_ref[0]


def _modulation(c_all, w_ada, b_ada):
    L, D, N = w_ada.shape
    R = c_all.shape[0]
    tn = _pick(N, MOD_TN)
    return pl.pallas_call(
        _mod_kernel,
        grid=(L, N // tn),
        in_specs=[pl.BlockSpec((R, D), lambda l, j: (0, 0)),
                  pl.BlockSpec((1, D, tn), lambda l, j: (l, 0, j)),
                  pl.BlockSpec((1, 1, tn), lambda l, j: (l, 0, j))],
        out_specs=pl.BlockSpec((1, R, tn), lambda l, j: (l, 0, j)),
        out_shape=jax.ShapeDtypeStruct((L, R, N), F32),
        compiler_params=_cparams(("parallel", "parallel")),
        name="adaln_mod",
    )(c_all.astype(BF16), w_ada, b_ada.reshape(L, 1, N))


def _h_kernel(x_ref, g_ref, sh_ref, sc_ref, o_ref):
    x = x_ref[0]
    r = lax.rsqrt(jnp.mean(x * x, axis=-1, keepdims=True) + EPS)
    y = x * r * g_ref[...]
    o_ref[0] = (y * (1.0 + sc_ref[0]) + sh_ref[0]).astype(BF16)


def _norm_modulate(x, ln_g, mod):
    B, S, D = x.shape
    ts = _pick(S, NORM_TS)
    return pl.pallas_call(
        _h_kernel,
        grid=(B, S // ts),
        in_specs=[pl.BlockSpec((1, ts, D), lambda b, i: (b, i, 0)),
                  pl.BlockSpec((1, D), lambda b, i: (0, 0)),
                  pl.BlockSpec((1, 1, D), lambda b, i: (b, 0, 0)),
                  pl.BlockSpec((1, 1, D), lambda b, i: (b, 0, 1))],
        out_specs=pl.BlockSpec((1, ts, D), lambda b, i: (b, i, 0)),
        out_shape=jax.ShapeDtypeStruct((B, S, D), BF16),
        compiler_params=_cparams(("parallel", "parallel")),
        name="norm_modulate",
    )(x, ln_g.reshape(1, D), mod, mod)


def _pair_layout(x):
    shp = x.shape
    x = x.reshape(shp[:-1] + (shp[-1] // LANES, 2, 2, ROPE_HALF))
    return jnp.swapaxes(x, -3, -2).reshape(shp)


def _rope128(y, cos, sin_signed):
    return y * cos + pltpu.roll(y, LANES // 2, 1) * sin_signed


def _rope_tables(pos_a, pos_b):
    freqs = ROPE_THETA ** (-jnp.arange(ROPE_HALF, dtype=F32) / ROPE_HALF)
    ang_a = pos_a[:, None] * freqs[None, :]
    ang_b = pos_b[:, None] * freqs[None, :]
    cos = jnp.concatenate([jnp.cos(ang_a), jnp.cos(ang_b)], axis=-1)
    sin = jnp.concatenate([jnp.sin(ang_a), jnp.sin(ang_b)], axis=-1)
    return jnp.concatenate([cos, cos], axis=-1), jnp.concatenate([-sin, sin], axis=-1)


def _mla_prep_kernel(cqn_ref, ckvn_ref, kr_ref, wq_ref, wkv_ref, gq_ref, gk_ref,
                     cos_ref, sin_ref, q_ref, k_ref, v_ref, *, rc):
    for r in range(cqn_ref.shape[0] // rc):
        _mla_prep_rows(slice(rc * r, rc * (r + 1)), cqn_ref, ckvn_ref, kr_ref, wq_ref, wkv_ref,
                       gq_ref, gk_ref, cos_ref, sin_ref, q_ref, k_ref, v_ref)


def _mla_prep_rows(rows, cqn_ref, ckvn_ref, kr_ref, wq_ref, wkv_ref, gq_ref, gk_ref,
                   cos_ref, sin_ref, q_ref, k_ref, v_ref):
    kr2 = kr_ref[rows, :]
    qraw = jnp.dot(cqn_ref[rows, :], wq_ref[...], preferred_element_type=F32)
    kvraw = jnp.dot(ckvn_ref[rows, :], wkv_ref[...], preferred_element_type=F32)
    cos = cos_ref[rows, :]
    sin = sin_ref[rows, :]
    lane = lax.broadcasted_iota(jnp.int32, cos.shape, 1)
    even = (lane % (2 * ROPE_HALF)) < ROPE_HALF
    gq = gq_ref[...]
    gk = gk_ref[...]
    nope_w = MLA_HEADS * MLA_NOPE
    inv_qk = 1.0 / MLA_QK
    kr_sq = kr2 * kr2
    k_rope = _rope128(kr2 * gk[:, LANES:], cos, sin)
    for p in range(MLA_HEADS // 2):
        qr = qraw[:, nope_w + LANES * p: nope_w + LANES * (p + 1)]
        qr_sq = qr * qr
        q_rope = _rope128(qr * gq[:, LANES:], cos, sin)
        for e in range(2):
            hd = 2 * p + e
            keep = even if e == 0 else jnp.logical_not(even)
            qn = qraw[:, LANES * hd: LANES * (hd + 1)]
            rq = lax.rsqrt(jnp.sum(qn * qn + jnp.where(keep, qr_sq, 0.0), axis=-1, keepdims=True) * inv_qk + EPS)
            q_ref[rows, 2 * LANES * hd: 2 * LANES * hd + LANES] = (qn * rq * gq[:, :LANES]).astype(BF16)
            q_ref[rows, 2 * LANES * hd + LANES: 2 * LANES * (hd + 1)] = jnp.where(keep, q_rope * rq, 0.0).astype(BF16)
            kn = kvraw[:, LANES * hd: LANES * (hd + 1)]
            rk = lax.rsqrt(jnp.sum(kn * kn + jnp.where(keep, kr_sq, 0.0), axis=-1, keepdims=True) * inv_qk + EPS)
            k_ref[rows, 2 * LANES * hd: 2 * LANES * hd + LANES] = (kn * rk * gk[:, :LANES]).astype(BF16)
            k_ref[rows, 2 * LANES * hd + LANES: 2 * LANES * (hd + 1)] = jnp.where(keep, k_rope * rk, 0.0).astype(BF16)
    v_ref[rows, :] = kvraw[:, nope_w:].astype(BF16)


def _mla_prep(lat, lat_col0, kr2, wq, wkv, gq, gk, cos, sin, S):
    T = lat.shape[0]
    tm = _pick(S, MLA_PREP_TM)
    nS = S // tm
    hq = MLA_HEADS * 2 * LANES
    const = lambda i: (0, 0)
    return pl.pallas_call(
        functools.partial(_mla_prep_kernel, rc=min(MLA_PREP_RC, tm)),
        grid=(T // tm,),
        in_specs=[pl.BlockSpec((tm, Q_LORA), lambda i: (i, lat_col0 // Q_LORA)),
                  pl.BlockSpec((tm, KV_LORA), lambda i: (i, (lat_col0 + Q_LORA) // KV_LORA)),
                  pl.BlockSpec((tm, LANES), lambda i: (i, 0)),
                  pl.BlockSpec(wq.shape, const),
                  pl.BlockSpec(wkv.shape, const),
                  pl.BlockSpec((1, 2 * LANES), const),
                  pl.BlockSpec((1, 2 * LANES), const),
                  pl.BlockSpec((tm, LANES), lambda i: (i % nS, 0)),
                  pl.BlockSpec((tm, LANES), lambda i: (i % nS, 0))],
        out_specs=[pl.BlockSpec((tm, hq), lambda i: (i, 0)),
                   pl.BlockSpec((tm, hq), lambda i: (i, 0)),
                   pl.BlockSpec((tm, MLA_HEADS * MLA_V), lambda i: (i, 0))],
        out_shape=[jax.ShapeDtypeStruct((T, hq), BF16),
                   jax.ShapeDtypeStruct((T, hq), BF16),
                   jax.ShapeDtypeStruct((T, MLA_HEADS * MLA_V), BF16)],
        compiler_params=_cparams(("parallel",)),
        name="mla_prep",
    )(lat, lat, kr2, wq, wkv, gq, gk, cos, sin)


QKV_KINDS = (0, 0, 0, 0, 0, 2, 1, 1, 1, 1, 1, 2)


def _qkv_kernel(h_ref, w_ref, gain_ref, cos_ref, sin_ref, o_ref, kr_ref, *, rc):
    tm = h_ref.shape[0]
    tn = 2 * LANES
    c0 = tn * len(QKV_KINDS)
    for r in range(tm // rc):
        rows = slice(rc * r, rc * (r + 1))
        for lo, width in ((c0, Q_LORA), (c0 + Q_LORA, KV_LORA)):
            x = jnp.dot(h_ref[rows, :], w_ref[:, lo: lo + width], preferred_element_type=F32)
            y = x * lax.rsqrt(jnp.mean(x * x, axis=-1, keepdims=True) + EPS) * gain_ref[:, lo: lo + width]
            o_ref[rows, lo: lo + width] = y.astype(BF16)
        kr_ref[rows, :] = jnp.dot(h_ref[rows, :], w_ref[:, c0 + Q_LORA + KV_LORA:], preferred_element_type=F32)
        for t, kind in enumerate(QKV_KINDS):
            acc = jnp.dot(h_ref[rows, :], w_ref[:, tn * t: tn * (t + 1)], preferred_element_type=F32)
            if kind == 2:
                o_ref[rows, tn * t: tn * (t + 1)] = acc.astype(BF16)
                continue
            for e in range(2):
                lanes = slice(tn * t + LANES * e, tn * t + LANES * (e + 1))
                x = acc[:, LANES * e: LANES * (e + 1)]
                y = x * lax.rsqrt(jnp.mean(x * x, axis=-1, keepdims=True) + EPS) * gain_ref[:, lanes]
                if kind == 0:
                    y = _rope128(y, cos_ref[rows, :], sin_ref[rows, :])
                o_ref[rows, lanes] = y.astype(BF16)


def _qkv_proj(h2, w, N, gains, cos, sin, S):
    T, D = h2.shape
    No = N - LANES
    tm = _pick(S, QKV_TM)
    nS = S // tm
    return pl.pallas_call(
        functools.partial(_qkv_kernel, rc=min(QKV_RC, tm)),
        grid=(T // tm,),
        in_specs=[pl.BlockSpec((tm, D), lambda i: (i, 0)),
                  pl.BlockSpec((D, N), lambda i: (0, 0), pipeline_mode=pl.Buffered(1)),
                  pl.BlockSpec((1, No), lambda i: (0, 0)),
                  pl.BlockSpec((tm, LANES), lambda i: (i % nS, 0)),
                  pl.BlockSpec((tm, LANES), lambda i: (i % nS, 0))],
        out_specs=[pl.BlockSpec((tm, No), lambda i: (i, 0)),
                   pl.BlockSpec((tm, LANES), lambda i: (i, 0))],
        out_shape=[jax.ShapeDtypeStruct((T, No), BF16),
                   jax.ShapeDtypeStruct((T, LANES), F32)],
        compiler_params=_cparams(("parallel",)),
        name="qkv_proj",
    )(h2, w, gains, cos, sin)


def _gate_kernel(h_ref, w_ref, o_ref, *, rc, tn):
    for r in range(h_ref.shape[0] // rc):
        rows = slice(rc * r, rc * (r + 1))
        for t in range(w_ref.shape[1] // tn):
            cols = slice(tn * t, tn * (t + 1))
            o_ref[rows, cols] = jnp.dot(h_ref[rows, :], w_ref[:, cols], preferred_element_type=F32)


def _gate_proj(h2, w, N, col0):
    T, D = h2.shape
    tm = _pick(T, GATE_TM)
    gw = N // GATE_GROUPS
    j0 = col0 // gw
    return pl.pallas_call(
        functools.partial(_gate_kernel, rc=min(GATE_RC, tm), tn=_pick(gw, GATE_TN)),
        grid=(GATE_GROUPS, T // tm),
        in_specs=[pl.BlockSpec((tm, D), lambda j, i: (i, 0)),
                  pl.BlockSpec((D, gw), lambda j, i: (0, j0 + j), pipeline_mode=pl.Buffered(1))],
        out_specs=pl.BlockSpec((tm, gw), lambda j, i: (i, j)),
        out_shape=jax.ShapeDtypeStruct((T, N), F32),
        compiler_params=_cparams(("parallel", "parallel")),
        name="gate_proj",
    )(h2, w)


def _flash_kernel(q_ref, k_ref, v_ref, g_ref, o_ref, vext_ref, *, G, dk, tq, tk, S, nt):
    qi = pl.program_id(2)

    def build_vext():
        lane = lax.broadcasted_iota(jnp.int32, (S, LANES), 1)
        vext_ref[:, :LANES] = v_ref[0]
        vext_ref[:, LANES:] = jnp.where(lane == 0, 1.0, 0.0).astype(BF16)

    if nt * tq == S:
        build_vext()
    else:
        pl.when(qi == 0)(build_vext)
    n = S // tk
    for t in range(nt):
        qrows = slice(tq * t, tq * (t + 1))
        if G == 1:
            q = q_ref[0, qrows, :]
        else:
            q = jnp.concatenate([q_ref[0, qrows, dk * g: dk * (g + 1)] for g in range(G)], axis=0)

        def scores(c, q=q):
            k = k_ref[0, tk * c: tk * (c + 1), :]
            return lax.dot_general(q, k, (((1,), (1,)), ((), ())), preferred_element_type=F32)

        s_next = scores(0)
        m = None
        acc = None
        for c in range(n):
            s = s_next
            if c + 1 < n:
                s_next = scores(c + 1)
            m_blk = jnp.max(s, axis=1, keepdims=True)
            m_new = m_blk if m is None else jnp.maximum(m, m_blk)
            p = jnp.exp2(s - m_new).astype(BF16)
            pv = jnp.dot(p, vext_ref[tk * c: tk * (c + 1), :], preferred_element_type=F32)
            acc = pv if acc is None else jnp.exp2(m - m_new) * acc + pv
            m = m_new
        o = acc[:, :LANES] / acc[:, LANES:LANES + 1]
        for g in range(G):
            gate = g_ref[0, qrows, LANES * g: LANES * (g + 1)]
            o_ref[0, qrows, LANES * g: LANES * (g + 1)] = (
                o[tq * g: tq * (g + 1)] * (gate * jax.nn.sigmoid(gate))).astype(BF16)


def _flash_attn(q_arr, q_blk0, k_arr, k_blk0, v_arr, v_blk0, g_arr, g_blk0, *, B, S, H, KV, dk, rows):
    G = H // KV
    tq = rows // G
    tk = _pick(S // 2, FLASH_TK)
    nt = min(FLASH_NT, S // tq)
    kern = functools.partial(_flash_kernel, G=G, dk=dk, tq=tq, tk=tk, S=S, nt=nt)
    return pl.pallas_call(
        kern,
        grid=(B, KV, S // (tq * nt)),
        in_specs=[pl.BlockSpec((1, tq * nt, G * dk), lambda b, h, i: (b, i, q_blk0 + h)),
                  pl.BlockSpec((1, S, dk), lambda b, h, i: (b, 0, k_blk0 + h)),
                  pl.BlockSpec((1, S, LANES), lambda b, h, i: (b, 0, v_blk0 + h)),
                  pl.BlockSpec((1, tq * nt, G * LANES), lambda b, h, i: (b, i, g_blk0 + h))],
        out_specs=pl.BlockSpec((1, tq * nt, G * LANES), lambda b, h, i: (b, i, h)),
        out_shape=jax.ShapeDtypeStruct((B, S, H * LANES), BF16),
        scratch_shapes=[pltpu.VMEM((S, 2 * LANES), BF16)],
        compiler_params=_cparams(("parallel", "parallel", "arbitrary")),
        name=f"flash_attn_g{G}",
    )(q_arr, k_arr, v_arr, g_arr)


def _win_kernel(q_ref, kp_ref, kc_ref, kn_ref, vp_ref, vc_ref, vn_ref, bias_ref, sink_ref, g_ref, o_ref, *, G, nq, nsteps):
    j = pl.program_id(2)
    nt = (((1,), (1,)), ((), ()))
    sink = sink_ref[0]

    def kv_block(prev_ref, cur_ref, next_ref, t):
        if t < 0:
            return prev_ref[0]
        if t >= nq:
            return next_ref[0]
        return cur_ref[0, BLOCK * t: BLOCK * (t + 1), :]

    for t in range(nq):
        rows = slice(BLOCK * t, BLOCK * (t + 1))
        q = jnp.concatenate([q_ref[0, rows, LANES * g: LANES * (g + 1)] for g in range(G)], axis=0)
        s = [lax.dot_general(q, kv_block(kp_ref, kc_ref, kn_ref, t - 1 + d), nt, preferred_element_type=F32)
             + bias_ref[0, :, BLOCK * d: BLOCK * (d + 1)] for d in range(3)]
        if t == 0:
            s[0] = s[0] + jnp.where(j == 0, NEG, 0.0)
        if t == nq - 1:
            s[2] = s[2] + jnp.where(j == nsteps - 1, NEG, 0.0)
        m = jnp.maximum(jnp.max(jnp.maximum(jnp.maximum(s[0], s[1]), s[2]), axis=1, keepdims=True), sink)
        p = [jnp.exp2(sd - m) for sd in s]
        l = jnp.sum(p[0] + p[1] + p[2], axis=1, keepdims=True) + jnp.exp2(sink - m)
        o = sum(jnp.dot(p[d].astype(BF16), kv_block(vp_ref, vc_ref, vn_ref, t - 1 + d), preferred_element_type=F32)
                for d in range(3)) / l
        for g in range(G):
            gate = g_ref[0, rows, LANES * g: LANES * (g + 1)]
            o_ref[0, rows, LANES * g: LANES * (g + 1)] = (
                o[BLOCK * g: BLOCK * (g + 1)] * (gate * jax.nn.sigmoid(gate))).astype(BF16)


def _window_attn(qkv, q_blk0, k_blk0, v_blk0, gate, g_blk0, bias, sink, *, B, S):
    G = WIN_HEADS // WIN_KV
    nb = S // BLOCK
    nq = next(n for n in WIN_NQ if nb % n == 0)
    nsteps = nb // nq
    kern = functools.partial(_win_kernel, G=G, nq=nq, nsteps=nsteps)

    def kv_specs(blk0):
        return [pl.BlockSpec((1, BLOCK, LANES), lambda b, h, j: (b, jnp.maximum(j * nq - 1, 0), blk0 + h)),
                pl.BlockSpec((1, nq * BLOCK, LANES), lambda b, h, j: (b, j, blk0 + h)),
                pl.BlockSpec((1, BLOCK, LANES), lambda b, h, j: (b, jnp.minimum((j + 1) * nq, nb - 1), blk0 + h))]

    return pl.pallas_call(
        kern,
        grid=(B, WIN_KV, nsteps),
        in_specs=[pl.BlockSpec((1, nq * BLOCK, G * LANES), lambda b, h, j: (b, j, q_blk0 + h))]
        + kv_specs(k_blk0) + kv_specs(v_blk0)
        + [pl.BlockSpec((1, G * BLOCK, 3 * BLOCK), lambda b, h, j: (h, 0, 0)),
           pl.BlockSpec((1, G * BLOCK, 1), lambda b, h, j: (h, 0, 0)),
           pl.BlockSpec((1, nq * BLOCK, G * LANES), lambda b, h, j: (b, j, g_blk0 + h))],
        out_specs=pl.BlockSpec((1, nq * BLOCK, G * LANES), lambda b, h, j: (b, j, h)),
        out_shape=jax.ShapeDtypeStruct((B, S, WIN_HEADS * LANES), BF16),
        compiler_params=_cparams(("parallel", "parallel", "arbitrary")),
        name="window_attn",
    )(qkv, qkv, qkv, qkv, qkv, qkv, qkv, bias, sink, gate)


def _t5_bucket(rel):
    half = N_BUCKETS // 2
    max_exact = half // 2
    ret = jnp.where(rel > 0, half, 0)
    n = jnp.abs(rel)
    nf = jnp.maximum(n, 1).astype(F32)
    large = max_exact + (jnp.log(nf / max_exact) / math.log(MAX_DIST / max_exact)
                         * (half - max_exact)).astype(jnp.int32)
    large = jnp.minimum(large, half - 1)
    return ret + jnp.where(n < max_exact, n, large)


def _bias_kernel(bucket_ref, rb_ref, o_ref):
    bucket = bucket_ref[...]
    row = lax.broadcasted_iota(jnp.int32, bucket.shape, 0)
    colk = lax.broadcasted_iota(jnp.int32, bucket.shape, 1)
    inband = jnp.abs(colk - WINDOW - row) <= WINDOW
    for h in range(WIN_HEADS):
        acc = jnp.zeros(bucket.shape, F32)
        for b in range(N_BUCKETS):
            acc = jnp.where(bucket == b, rb_ref[b, h], acc)
        o_ref[h] = jnp.where(inband, acc * LOG2E, NEG)


def _window_bias_table(rel_bias):
    span = BLOCK + 2 * WINDOW
    rel = jnp.arange(span)[None, :] - WINDOW - jnp.arange(BLOCK)[:, None]
    bias = pl.pallas_call(
        _bias_kernel,
        in_specs=[pl.BlockSpec(memory_space=pltpu.VMEM), pl.BlockSpec(memory_space=pltpu.SMEM)],
        out_specs=pl.BlockSpec(memory_space=pltpu.VMEM),
        out_shape=jax.ShapeDtypeStruct((WIN_HEADS, BLOCK, span), F32),
        name="t5_bias_table",
    )(_t5_bucket(rel).astype(jnp.int32), rel_bias.astype(F32))
    G = WIN_HEADS // WIN_KV
    return bias.reshape(WIN_KV, G * BLOCK, span)


def _branch_kernel(*refs, tn):
    a, w, m, o_ref = refs[:3], refs[3:6], refs[6:-1], refs[-1]
    nj = o_ref.shape[1] // tn
    for j in range(nj):
        cols = slice(tn * j, tn * (j + 1))
        y = None
        for k in range(3):
            t = jax.nn.sigmoid(m[nj * k + j][...]) * jnp.dot(a[k][...], w[k][:, cols], preferred_element_type=F32)
            y = t if y is None else y + t
        o_ref[:, cols] = y.astype(BF16)


def _branch_merge(a_list, w_list, gate, m_col0):
    T, W = a_list[0].shape
    D = w_list[0].shape[1]
    tm = _pick(T, BR_TM)
    tn = _pick(D, BR_TN)
    nj = D // tn
    a_spec = pl.BlockSpec((tm, W), lambda i: (i, 0))
    w_spec = pl.BlockSpec((W, D), lambda i: (0, 0), pipeline_mode=pl.Buffered(1))

    def m_spec(k, j):
        blk = (m_col0 + k * D) // tn + j
        return pl.BlockSpec((tm, tn), lambda i: (i, blk))

    assert m_col0 % tn == 0
    m_specs = [m_spec(k, j) for k in range(3) for j in range(nj)]
    return pl.pallas_call(
        functools.partial(_branch_kernel, tn=tn),
        grid=(T // tm,),
        in_specs=[a_spec] * 3 + [w_spec] * 3 + m_specs,
        out_specs=pl.BlockSpec((tm, D), lambda i: (i, 0)),
        out_shape=jax.ShapeDtypeStruct((T, D), BF16),
        compiler_params=_cparams(("parallel",)),
        name="branch_merge",
    )(*a_list, *w_list, *([gate] * (3 * nj)))


def _out_kernel(y_ref, w_ref, x_ref, gt_ref, *rest, rc, with_next):
    if with_next:
        g_ref, sh_ref, sc_ref, o_ref, hn_ref = rest
    else:
        (o_ref,) = rest
    for r in range(y_ref.shape[1] // rc):
        rows = slice(rc * r, rc * (r + 1))
        x = x_ref[0, rows, :] + gt_ref[0] * jnp.dot(y_ref[0, rows, :], w_ref[...], preferred_element_type=F32)
        o_ref[0, rows, :] = x
        if with_next:
            rms = lax.rsqrt(jnp.mean(x * x, axis=-1, keepdims=True) + EPS)
            hn_ref[0, rows, :] = ((x * rms * g_ref[...]) * (1.0 + sc_ref[0]) + sh_ref[0]).astype(BF16)


def _out_proj(y, w_out, x, mod, ln_g_next=None, mod_next=None):
    B, S, D = x.shape
    tm = _pick(S, OUT_TM)
    with_next = ln_g_next is not None
    row_spec = pl.BlockSpec((1, tm, D), lambda b, i: (b, i, 0))
    in_specs = [row_spec,
                pl.BlockSpec((D, D), lambda b, i: (0, 0)),
                row_spec,
                pl.BlockSpec((1, 1, D), lambda b, i: (b, 0, 2))]
    args = [y, w_out, x, mod]
    out_specs = [row_spec]
    out_shape = [jax.ShapeDtypeStruct((B, S, D), F32)]
    if with_next:
        in_specs += [pl.BlockSpec((1, D), lambda b, i: (0, 0)),
                     pl.BlockSpec((1, 1, D), lambda b, i: (b, 0, 0)),
                     pl.BlockSpec((1, 1, D), lambda b, i: (b, 0, 1))]
        args += [ln_g_next.reshape(1, D), mod_next, mod_next]
        out_specs.append(row_spec)
        out_shape.append(jax.ShapeDtypeStruct((B, S, D), BF16))
    res = pl.pallas_call(
        functools.partial(_out_kernel, rc=min(OUT_RC, tm), with_next=with_next),
        grid=(B, S // tm),
        in_specs=in_specs,
        out_specs=out_specs,
        out_shape=out_shape,
        compiler_params=_cparams(("parallel", "parallel")),
        name="out_proj",
    )(*args)
    return (res[0], res[1]) if with_next else (res[0], None)


def _split_points(D):
    sizes = [Q_LORA, KV_LORA, MLA_ROPE, MLA_HEADS * MLA_V,
             AX_HEADS * AX_HD, AX_KV * AX_HD, AX_KV * AX_HD, AX_HEADS * AX_HD,
             WIN_HEADS * WIN_HD, WIN_KV * WIN_HD, WIN_KV * WIN_HD, WIN_HEADS * WIN_HD,
             D, D, D]
    pts, acc = [0], 0
    for s in sizes:
        acc += s
        pts.append(acc)
    return pts


def _prep_weights(D, w_in, w_q_up, w_kv_up, q_a_norm, kv_a_norm, mla_qn, mla_kn,
                  ax_qn, ax_kn, win_qn, win_kn, w_br_mla, w_br_ax, w_br_win, w_out):
    p = _split_points(D)
    col = lambda k: w_in[:, :, p[k]:p[k + 1]]
    (cq, ckv, kr, g_mla, aq, ak, av, g_ax, wq, wk, wv, g_win, m_mla, m_ax, m_win) = [col(k) for k in range(15)]
    L = w_in.shape[0]
    w = {}
    qkv_cols = [_pair_layout(aq), _pair_layout(ak), av, wq, wk, wv, cq, ckv,
                _pair_layout(jnp.concatenate([kr, kr], axis=-1))]
    gate_cols = [g_mla, g_ax, g_win, m_mla, m_ax, m_win]
    n_qkv = sum(c.shape[-1] for c in qkv_cols)
    n_gate = sum(c.shape[-1] for c in gate_cols)
    gw = n_gate // GATE_GROUPS
    col0 = -(-n_qkv // gw) * gw
    pad = jnp.zeros(w_in.shape[:2] + (col0 - n_qkv,), w_in.dtype)
    w["in"] = jnp.concatenate(qkv_cols + [pad] + gate_cols, axis=-1).astype(BF16)
    w["n_qkv"] = n_qkv
    w["n_gate"] = n_gate
    w["gate_col0"] = col0
    wq4 = w_q_up.reshape(L, Q_LORA, MLA_HEADS, MLA_QK)
    w["q_up"] = jnp.concatenate([wq4[..., :MLA_NOPE].reshape(L, Q_LORA, -1),
                                 _pair_layout(wq4[..., MLA_NOPE:].reshape(L, Q_LORA, -1))], axis=-1).astype(BF16)
    wkv4 = w_kv_up.reshape(L, KV_LORA, MLA_HEADS, MLA_NOPE + MLA_V)
    w["kv_up"] = jnp.concatenate([wkv4[..., :MLA_NOPE].reshape(L, KV_LORA, -1),
                                  wkv4[..., MLA_NOPE:].reshape(L, KV_LORA, -1)], axis=-1).astype(BF16)
    qs_mla = MLA_QK ** -0.5 * LOG2E
    pair_gain = lambda g: jnp.concatenate(
        [g[:, :MLA_NOPE], _pair_layout(jnp.concatenate([g[:, MLA_NOPE:], g[:, MLA_NOPE:]], axis=-1))], axis=-1)
    w["gq"] = (pair_gain(mla_qn) * qs_mla).reshape(L, 1, 2 * LANES)
    w["gk"] = pair_gain(mla_kn).reshape(L, 1, 2 * LANES)
    ones = jnp.ones((L, AX_KV * AX_HD), F32)
    w["qkv_gain"] = jnp.concatenate(
        [jnp.tile(_pair_layout(ax_qn), (1, AX_HEADS)) * (AX_HD ** -0.5 * LOG2E),
         jnp.tile(_pair_layout(ax_kn), (1, AX_KV)), ones,
         jnp.tile(win_qn, (1, WIN_HEADS)) * (WIN_HD ** -0.5 * LOG2E), jnp.tile(win_kn, (1, WIN_KV)), ones,
         q_a_norm, kv_a_norm], axis=-1).reshape(L, 1, -1)
    w["br"] = [w_br_mla.astype(BF16), w_br_ax.astype(BF16), w_br_win.astype(BF16)]
    w["out"] = w_out.astype(BF16)
    return w


def _layer(x, h, mod, mod_next, l, ln_g, w, tabs, bias_tab, sink_tab):
    B, S, D = x.shape
    T = B * S
    h2 = h.reshape(T, D)
    cos_m, sin_m, cos_a, sin_a = tabs[S]
    qkv, kr2 = _qkv_proj(h2, w["in"][l], w["n_qkv"], w["qkv_gain"][l], cos_a, sin_a, S)
    q_m, k_m, v_m = _mla_prep(qkv, 2 * LANES * len(QKV_KINDS), kr2, w["q_up"][l], w["kv_up"][l],
                              w["gq"][l], w["gk"][l], cos_m, sin_m, S)
    gate = _gate_proj(h2, w["in"][l], w["n_gate"], w["gate_col0"])
    r3 = lambda a: a.reshape(B, S, a.shape[-1])
    q_m, k_m, v_m, qkv3, gate3 = r3(q_m), r3(k_m), r3(v_m), r3(qkv), r3(gate)
    a_mla = _flash_attn(q_m, 0, k_m, 0, v_m, 0, gate3, 0,
                        B=B, S=S, H=MLA_HEADS, KV=MLA_HEADS, dk=2 * LANES, rows=min(FLASH_ROWS, S))
    gq = AX_HEADS // AX_KV
    a_ax = _flash_attn(qkv3, 0, qkv3, AX_HEADS, qkv3, AX_HEADS + AX_KV, gate3, AX_KV,
                       B=B, S=S, H=AX_HEADS, KV=AX_KV, dk=AX_HD, rows=min(FLASH_ROWS, S))
    wbase = AX_HEADS + 2 * AX_KV
    a_win = _window_attn(qkv3, wbase // gq, wbase + WIN_HEADS, wbase + WIN_HEADS + WIN_KV,
                         gate3, 2 * AX_KV, bias_tab, sink_tab[l], B=B, S=S)
    a_list = [a.reshape(T, a.shape[-1]) for a in (a_mla, a_ax, a_win)]
    y = _branch_merge(a_list, [wb[l] for wb in w["br"]], gate, 3 * MLA_HEADS * MLA_V)
    if mod_next is None:
        return _out_proj(y.reshape(B, S, D), w["out"][l], x, mod)
    return _out_proj(y.reshape(B, S, D), w["out"][l], x, mod, ln_g[l + 1], mod_next)


def kernel(x_prompt, x_sample, c_prompt, c_sample, ln_g, w_ada, b_ada, w_in, q_a_norm, w_q_up, kv_a_norm, w_kv_up, mla_qn, mla_kn, ax_qn, ax_kn, win_qn, win_kn, win_sink, rel_bias, w_br_mla, w_br_ax, w_br_win, w_out):
    L = ln_g.shape[0]
    D = x_prompt.shape[-1]
    Bp = x_prompt.shape[0]
    w = _prep_weights(D, w_in, w_q_up, w_kv_up, q_a_norm, kv_a_norm, mla_qn, mla_kn,
                      ax_qn, ax_kn, win_qn, win_kn, w_br_mla, w_br_ax, w_br_win, w_out)
    mod_all = _modulation(jnp.concatenate([c_prompt, c_sample], axis=0), w_ada, b_ada)
    tabs = {}
    for S in {x_prompt.shape[1], x_sample.shape[1]}:
        pos = jnp.arange(S, dtype=F32)
        row = jnp.repeat(jnp.arange(S // GRID_W, dtype=F32), GRID_W)
        colp = jnp.tile(jnp.arange(GRID_W, dtype=F32), S // GRID_W)
        tabs[S] = _rope_tables(pos, pos) + _rope_tables(row, colp)
    bias_tab = _window_bias_table(rel_bias)
    G = WIN_HEADS // WIN_KV
    sink_tab = jnp.repeat(win_sink.astype(F32) * LOG2E, BLOCK, axis=-1).reshape(L, WIN_KV, G * BLOCK, 1)
    mods_p = [mod_all[l, :Bp][:, None, :] for l in range(L)]
    mods_s = [mod_all[l, Bp:][:, None, :] for l in range(L)]
    yp, ys = x_prompt, x_sample
    hp = _norm_modulate(yp, ln_g[0], mods_p[0])
    hs = _norm_modulate(ys, ln_g[0], mods_s[0])
    for l in range(L):
        nxt = l + 1 < L
        yp, hp = _layer(yp, hp, mods_p[l], mods_p[l + 1] if nxt else None, l, ln_g, w, tabs, bias_tab, sink_tab)
        ys, hs = _layer(ys, hs, mods_s[l], mods_s[l + 1] if nxt else None, l, ln_g, w, tabs, bias_tab, sink_tab)
    return (yp, ys)
```
